```python
import jax, jax.numpy as jnp
from jax import lax
import numpy as np


D_MODEL = 2048
BATCH = 2
SEQ = 16384
DEPTH = 1
DEC_BATCH = 2
DEC_SEQ = 8192
PAST_LEN = 128

HEAD_DIM = 128
N_HEADS_NA = 4
NA_ROWS_MAX = 8
NA_KW = 16
GRID_W = 64
DIL_GROUPS = ((128, 1), (512, 4), (2048, 16))
N_HEADS_PER_DIL = 4
N_HEADS_DIL = N_HEADS_PER_DIL * len(DIL_GROUPS)
D_NA = N_HEADS_NA * HEAD_DIM
D_DIL = N_HEADS_DIL * HEAD_DIM
D_DIL_OUT = N_HEADS_PER_DIL * HEAD_DIM
D_IN = 3 * D_NA + 3 * D_DIL + 2 * D_MODEL
D_FF = 4 * D_MODEL
ALIBI_MAX_BIAS = 8.0
NORM_EPS = 1e-6
NEG_INF = -1e30

kernel_name = 'hybrid_natten_dilated_encoder'


def rms_norm(x, g):
    xf = x.astype(jnp.float32)
    y = xf * lax.rsqrt(jnp.mean(xf * xf, axis=-1, keepdims=True) + NORM_EPS)
    return (y * g.astype(jnp.float32)).astype(x.dtype)


def alibi_slopes(n_heads):
    h = np.arange(1, n_heads + 1, dtype=np.float32)
    return jnp.asarray(np.power(np.float32(2.0), -ALIBI_MAX_BIAS * h / n_heads), dtype=jnp.float32)


def neighbourhood_attention(q, k, v, rpb):
    b, seq_len, h, dh = q.shape
    rows = seq_len // GRID_W
    kh = min(NA_ROWS_MAX, rows)
    qg = q.reshape(b, rows, GRID_W, h, dh)
    kg = k.reshape(b, rows, GRID_W, h, dh)
    vg = v.reshape(b, rows, GRID_W, h, dh)
    r = np.arange(rows)
    row_start = np.clip(r - kh // 2, 0, rows - kh)
    key_rows = row_start[:, None] + np.arange(kh)[None, :]
    row_idx = key_rows - r[:, None] + NA_ROWS_MAX - 1
    k_rows = kg[:, key_rows]
    v_rows = vg[:, key_rows]
    scale = HEAD_DIM ** -0.5
    outs = []
    for j in range(GRID_W // NA_KW):
        q0 = j * NA_KW
        k0 = min(max(q0 - NA_KW // 2, 0), GRID_W - 2 * NA_KW)
        qc = np.arange(q0, q0 + NA_KW)
        kc = np.arange(k0, k0 + 2 * NA_KW)
        col_start = np.clip(qc - NA_KW // 2, 0, GRID_W - NA_KW)
        in_win = (kc[None, :] >= col_start[:, None]) & (kc[None, :] < col_start[:, None] + NA_KW)
        col_idx = np.clip(kc[None, :] - qc[:, None] + NA_KW - 1, 0, 2 * NA_KW - 2)
        bias = rpb[:, row_idx[:, None, :, None], col_idx[None, :, None, :]].astype(jnp.float32)
        bias = jnp.where(in_win[None, None, :, None, :], bias, NEG_INF)
        kb = k_rows[:, :, :, k0:k0 + 2 * NA_KW]
        vb = v_rows[:, :, :, k0:k0 + 2 * NA_KW]
        s = jnp.einsum('brqhd,brikhd->bhrqik', qg[:, :, q0:q0 + NA_KW], kb).astype(jnp.float32) * scale + bias[None]
        p = jax.nn.softmax(s, axis=(-2, -1))
        outs.append(jnp.einsum('bhrqik,brikhd->brqhd', p.astype(v.dtype), vb))
    return jnp.concatenate(outs, axis=2).reshape(b, seq_len, h, dh)


def dilated_window_attention(q, k, v, slopes, window, dilation):
    b, seq_len, h, dh = q.shape
    n = window // (2 * dilation)
    ls = seq_len // dilation
    nb = -(-ls // n)
    lp = nb * n

    def by_residue(x):
        return x.reshape(b, ls, dilation, h, dh).transpose(0, 2, 1, 3, 4)

    qb = jnp.pad(by_residue(q), ((0, 0), (0, 0), (0, lp - ls), (0, 0), (0, 0))).reshape(b, dilation, nb, n, h, dh)
    pad_kv = ((0, 0), (0, 0), (n, lp - ls + n), (0, 0), (0, 0))

    def windows(x):
        xb = jnp.pad(by_residue(x), pad_kv).reshape(b, dilation, nb + 2, n, h, dh)
        return jnp.concatenate([xb[:, :, :-2], xb[:, :, 1:-1], xb[:, :, 2:]], axis=3)

    kw, vw = windows(k), windows(v)
    delta = np.arange(3 * n)[None, :] - n - np.arange(n)[:, None]
    m_key = np.arange(nb)[:, None] * n + np.arange(3 * n)[None, :] - n
    valid = (np.abs(delta) <= n)[None] & ((m_key >= 0) & (m_key < ls))[:, None, :]
    bias = -slopes[:, None, None] * jnp.asarray(np.abs(delta) * dilation, jnp.float32)
    s = jnp.einsum('benqhd,benkhd->benhqk', qb, kw).astype(jnp.float32) * (HEAD_DIM ** -0.5) + bias
    s = jnp.where(valid[None, None, :, None], s, NEG_INF)
    lse = jax.nn.logsumexp(s, axis=-1)
    p = jnp.exp(s - lse[..., None])
    o = jnp.einsum('benhqk,benkhd->benqhd', p.astype(v.dtype), vw)
    o = o.reshape(b, dilation, lp, h, dh)[:, :, :ls].transpose(0, 2, 1, 3, 4).reshape(b, seq_len, h, dh)
    lse = lse.transpose(0, 1, 2, 4, 3).reshape(b, dilation, lp, h)[:, :, :ls].transpose(0, 2, 1, 3).reshape(b, seq_len, h)
    return o, lse


def token_mixing(u, w_in, rpb, w_na_out, w_dil_out, w_o):
    b, seq_len, _ = u.shape
    z = u @ w_in
    sizes = [D_NA] * 3 + [D_DIL] * 3 + [D_MODEL]
    q_na, k_na, v_na, q_dl, k_dl, v_dl, g_na, g_dl = jnp.split(z, np.cumsum(sizes).tolist(), axis=-1)
    hs_na = lambda t: t.reshape(b, seq_len, N_HEADS_NA, HEAD_DIM)
    o_na = neighbourhood_attention(hs_na(q_na), hs_na(k_na), hs_na(v_na), rpb)
    hs_dl = lambda t: t.reshape(b, seq_len, N_HEADS_DIL, HEAD_DIM)
    qd, kd, vd = hs_dl(q_dl), hs_dl(k_dl), hs_dl(v_dl)
    slopes = alibi_slopes(N_HEADS_DIL)
    outs, lses = [], []
    for g, (window, dilation) in enumerate(DIL_GROUPS):
        sl = slice(g * N_HEADS_PER_DIL, (g + 1) * N_HEADS_PER_DIL)
        o_g, lse_g = dilated_window_attention(qd[:, :, sl], kd[:, :, sl], vd[:, :, sl], slopes[sl], window, dilation)
        outs.append(o_g)
        lses.append(lse_g)
    alpha = jax.nn.softmax(jnp.stack(lses, axis=0), axis=0)
    o_dl = jnp.sum(alpha[..., None] * jnp.stack(outs, axis=0).astype(jnp.float32), axis=0).astype(u.dtype)
    y_na = o_na.reshape(b, seq_len, D_NA) @ w_na_out
    y_dl = o_dl.reshape(b, seq_len, D_DIL_OUT) @ w_dil_out
    merged = jax.nn.sigmoid(g_na) * y_na + jax.nn.sigmoid(g_dl) * y_dl
    return merged @ w_o


def encoder_layer(x, c, w_mod, b_mod, g_mix, w_in, rpb, w_na_out, w_dil_out, w_o, g_ffn, w_ff1, w_ff2):
    mod = (jax.nn.silu(c) @ w_mod + b_mod)[:, None, :]
    sh1, sc1, gt1, sh2, sc2, gt2 = jnp.split(mod, 6, axis=-1)
    u = rms_norm(x, g_mix) * (1 + sc1) + sh1
    x = x + gt1 * token_mixing(u, w_in, rpb, w_na_out, w_dil_out, w_o)
    u = rms_norm(x, g_ffn) * (1 + sc2) + sh2
    hid = jnp.square(jax.nn.relu(u @ w_ff1))
    return x + gt2 * (hid @ w_ff2)


def setup_inputs(seed: int = 0) -> dict:
    key = jax.random.key(seed)
    ks = jax.random.split(key, 20)
    f32 = jnp.float32

    def nrm(k, shape, s):
        return jax.random.normal(k, shape, f32) * s

    return {
        'x_prompt': nrm(ks[0], (BATCH, SEQ, D_MODEL), 1.0),
        'x_sample': nrm(ks[1], (DEC_BATCH, DEC_SEQ, D_MODEL), 1.0),
        'c_prompt': nrm(ks[2], (BATCH, D_MODEL), 1.0),
        'c_sample': nrm(ks[3], (DEC_BATCH, D_MODEL), 1.0),
        'w_mod': nrm(ks[4], (DEPTH, D_MODEL, 6 * D_MODEL), 0.5 * D_MODEL ** -0.5),
        'b_mod': nrm(ks[5], (DEPTH, 6 * D_MODEL), 0.02),
        'g_mix': 1.0 + nrm(ks[6], (DEPTH, D_MODEL), 0.02),
        'w_in': nrm(ks[7], (DEPTH, D_MODEL, D_IN), D_MODEL ** -0.5),
        'rpb': nrm(ks[8], (DEPTH, N_HEADS_NA, 2 * NA_ROWS_MAX - 1, 2 * NA_KW - 1), 0.1),
        'w_na_out': nrm(ks[9], (DEPTH, D_NA, D_MODEL), D_NA ** -0.5),
        'w_dil_out': nrm(ks[10], (DEPTH, D_DIL_OUT, D_MODEL), D_DIL_OUT ** -0.5),
        'w_o': nrm(ks[11], (DEPTH, D_MODEL, D_MODEL), D_MODEL ** -0.5),
        'g_ffn': 1.0 + nrm(ks[12], (DEPTH, D_MODEL), 0.02),
        'w_ff1': nrm(ks[13], (DEPTH, D_MODEL, D_FF), D_MODEL ** -0.5),
        'w_ff2': nrm(ks[14], (DEPTH, D_FF, D_MODEL), D_FF ** -0.5),
        'g_final': 1.0 + nrm(ks[15], (D_MODEL,), 0.02),
    }


def reference(x_prompt, x_sample, c_prompt, c_sample, w_mod, b_mod, g_mix, w_in, rpb, w_na_out, w_dil_out, w_o, g_ffn, w_ff1, w_ff2, g_final):
    def trunk(x, c):
        for l in range(DEPTH):
            x = encoder_layer(x, c, w_mod[l], b_mod[l], g_mix[l], w_in[l], rpb[l], w_na_out[l], w_dil_out[l], w_o[l], g_ffn[l], w_ff1[l], w_ff2[l])
        return rms_norm(x, g_final)

    y_prompt = trunk(x_prompt, c_prompt)
    y_sample = trunk(x_sample, c_sample)
    return (y_prompt, y_sample)
```

```python
import functools

import numpy as np
import jax
import jax.numpy as jnp
from jax import lax
from jax.experimental import pallas as pl
from jax.experimental.pallas import tpu as pltpu

F32 = jnp.float32
BF16 = jnp.bfloat16

HEAD_DIM = 128
N_HEADS_NA = 4
NA_ROWS = 8
NA_KW = 16
GRID_W = 64
DIL_GROUPS = ((128, 1), (512, 4), (2048, 16))
N_HEADS_PER_DIL = 4
N_HEADS_DIL = N_HEADS_PER_DIL * len(DIL_GROUPS)
ALIBI_MAX_BIAS = 8.0
NORM_EPS = 1e-6
NEG_INF = -1e30
ATTN_SCALE = HEAD_DIM ** -0.5

VMEM_LIMIT_BYTES = 60 * 1024 * 1024

ZH_G_NA, ZH_G_DL = 0, 16
ZH_Q_NA, ZH_K_NA, ZH_V_NA = 32, 36, 40
ZH_Q_DL = (44, 56, 68)
ZH_K_DL = (48, 60, 72)
ZH_V_DL = (52, 64, 76)
Z_HEADS = 80

IN_TM = 1024
IN_TN = 2048
IN_TILE_KINDS = ((0, 0, 0, 0), (0, 0, 0, 0), (0, 0, 0, 0), (0, 0, 1, 1), (1, 2, 2, 2))

NA_TQ = 1024
NA_GROUP = 4 * GRID_W
NA_KEYS = 12 * GRID_W
DIL_TQ = 2048
DIL_Q = 128
DIL_K = 256
OUT_TM = 512
FFN_TM = 512
FFN_TF = 1024


def _params(*sem):
    return pltpu.CompilerParams(dimension_semantics=sem, vmem_limit_bytes=VMEM_LIMIT_BYTES)


def _w_in_column_order(d_model):
    d_na = N_HEADS_NA * HEAD_DIM
    d_dil = N_HEADS_DIL * HEAD_DIM
    d_grp = N_HEADS_PER_DIL * HEAD_DIM
    q_na, k_na, v_na = 0, d_na, 2 * d_na
    q_dl, k_dl, v_dl = 3 * d_na, 3 * d_na + d_dil, 3 * d_na + 2 * d_dil
    g_na = 3 * d_na + 3 * d_dil
    g_dl = g_na + d_model
    order = [(g_na, d_model), (g_dl, d_model), (q_na, d_na), (k_na, d_na), (v_na, d_na)]
    for g in range(len(DIL_GROUPS)):
        order += [(q_dl + g * d_grp, d_grp), (k_dl + g * d_grp, d_grp), (v_dl + g * d_grp, d_grp)]
    return order


def _mod_body(c_ref, w_ref, b_ref, o_ref):
    c = c_ref[...]
    s = (c * jax.nn.sigmoid(c)).astype(BF16)
    o_ref[...] = jnp.dot(s, w_ref[...].astype(BF16), preferred_element_type=F32) + b_ref[...]


def _modulation(c, w_mod, b_mod):
    rows, d = c.shape
    n = w_mod.shape[1]
    tn = 1024
    return pl.pallas_call(
        _mod_body,
        grid=(n // tn,),
        in_specs=[pl.BlockSpec((rows, d), lambda j: (0, 0)),
                  pl.BlockSpec((d, tn), lambda j: (0, j)),
                  pl.BlockSpec((1, tn), lambda j: (0, j))],
        out_specs=pl.BlockSpec((rows, tn), lambda j: (0, j)),
        out_shape=jax.ShapeDtypeStruct((rows, n), F32),
        compiler_params=_params("parallel"),
        name="modulation",
    )(c, w_mod, b_mod.reshape(1, n))


def _modulated_norm(x, g, sc, sh):
    ms = jnp.mean(x * x, axis=-1, keepdims=True)
    y = x * lax.rsqrt(ms + NORM_EPS) * g
    return y * (1.0 + sc) + sh


def _masked_softmax_pv(s, v):
    m = jnp.max(s, axis=-1, keepdims=True)
    p = jnp.exp(s - m)
    l = jnp.sum(p, axis=-1, keepdims=True)
    o = jnp.dot(p.astype(BF16), v, preferred_element_type=F32) / l
    return o, m + jnp.log(l)


def _qk(q, k):
    return lax.dot_general(q, k, (((1,), (1,)), ((), ())), preferred_element_type=F32)


def _in_proj_body(x_ref, sh_ref, sc_ref, g_ref, w_ref, z_ref, u_ref, p_ref):
    j = pl.program_id(2)
    tm = u_ref.shape[0]
    rows = 64

    @pl.when(j == 0)
    def _():
        def chunk(r, carry):
            r0 = pl.multiple_of(r * rows, rows)
            u = _modulated_norm(x_ref[0, pl.ds(r0, rows), :], g_ref[...], sc_ref[0], sh_ref[0])
            u_ref[pl.ds(r0, rows), :] = u.astype(BF16)
            return carry
        lax.fori_loop(0, tm // rows, chunk, 0)

    def subtile(s, kind):
        acc = jnp.dot(u_ref[...], w_ref[:, s * 512:(s + 1) * 512], preferred_element_type=F32)
        for hh in range(4):
            a = acc[:, hh * HEAD_DIM:(hh + 1) * HEAD_DIM]
            zh = s * 4 + hh
            if kind == 0:
                z_ref[0, zh] = a.astype(BF16)
                continue
            p_ref[hh] = a
            if kind == 1:
                for sb in range(tm // 256):
                    for e in range(4):
                        r0 = sb * 256 + e * 64
                        z_ref[0, zh, r0:r0 + 64, :] = p_ref[hh, pl.ds(sb * 256 + e, 64, stride=4), :].astype(BF16)
            else:
                for sb in range(tm // 1024):
                    for e in range(16):
                        r0 = sb * 1024 + e * 64
                        z_ref[0, zh, r0:r0 + 64, :] = p_ref[hh, pl.ds(sb * 1024 + e, 64, stride=16), :].astype(BF16)

    @pl.when(j < 3)
    def _():
        for s in range(4):
            subtile(s, 0)

    for jj in (3, 4):
        @pl.when(j == jj)
        def _(jj=jj):
            for s in range(4):
                subtile(s, IN_TILE_KINDS[jj][s])


def _in_proj(x, sh, sc, g, w_in_r):
    b, l, d = x.shape
    tm = min(IN_TM, l)
    n_tiles = w_in_r.shape[1] // IN_TN
    assert n_tiles == len(IN_TILE_KINDS) and l % tm == 0 and tm % 1024 == 0
    return pl.pallas_call(
        _in_proj_body,
        grid=(b, l // tm, n_tiles),
        in_specs=[pl.BlockSpec((1, tm, d), lambda bi, i, j: (bi, i, 0)),
                  pl.BlockSpec((1, 1, d), lambda bi, i, j: (bi, 0, 0)),
                  pl.BlockSpec((1, 1, d), lambda bi, i, j: (bi, 0, 0)),
                  pl.BlockSpec((1, d), lambda bi, i, j: (0, 0)),
                  pl.BlockSpec((d, IN_TN), lambda bi, i, j: (0, j))],
        out_specs=pl.BlockSpec((1, IN_TN // HEAD_DIM, tm, HEAD_DIM), lambda bi, i, j: (bi, j, i, 0)),
        out_shape=jax.ShapeDtypeStruct((b, Z_HEADS, l, HEAD_DIM), BF16),
        scratch_shapes=[pltpu.VMEM((tm, d), BF16), pltpu.VMEM((4, tm, HEAD_DIM), F32)],
        compiler_params=_params("parallel", "parallel", "arbitrary"),
        name="in_proj",
    )(x, sh, sc, g, w_in_r)


def _na_bias_tables(rpb):
    rho = np.arange(4)[:, None, None, None]
    qc = np.arange(GRID_W)[None, :, None, None]
    f = np.arange(12)[None, None, :, None]
    kc = np.arange(GRID_W)[None, None, None, :]
    col_start = np.clip(qc - NA_KW // 2, 0, GRID_W - NA_KW)
    col_ok = (kc >= col_start) & (kc < col_start + NA_KW)
    row_idx = np.clip(f - rho + 3, 0, 2 * NA_ROWS - 2)
    col_idx = np.clip(kc - qc + NA_KW - 1, 0, 2 * NA_KW - 2)
    row_ok = np.stack([
        (f >= rho) & (f < rho + NA_ROWS),
        (f >= 4) & (f < 4 + NA_ROWS) & (rho >= 0),
        (f >= 0) & (f < NA_ROWS) & (rho >= 0),
    ])
    ok = row_ok & col_ok[None]
    row_idx_b = np.broadcast_to(row_idx, (4, GRID_W, 12, GRID_W))
    col_idx_b = np.broadcast_to(col_idx, (4, GRID_W, 12, GRID_W))
    bias = rpb.astype(F32)[:, row_idx_b, col_idx_b]
    tab = jnp.where(ok[:, None], bias[None], NEG_INF)
    return tab.reshape(3, rpb.shape[0], NA_GROUP, NA_KEYS)


def _na_body(q_ref, k_ref, kp_ref, kn_ref, v_ref, vp_ref, vn_ref, tab_ref, o_ref, kbuf, vbuf, *, n_groups):
    i = pl.program_id(2)
    tq = q_ref.shape[2]
    halo = kp_ref.shape[2]
    kbuf[0:halo, :] = kp_ref[0, 0]
    kbuf[halo:halo + tq, :] = k_ref[0, 0]
    kbuf[halo + tq:, :] = kn_ref[0, 0]
    vbuf[0:halo, :] = vp_ref[0, 0]
    vbuf[halo:halo + tq, :] = v_ref[0, 0]
    vbuf[halo + tq:, :] = vn_ref[0, 0]

    def group(a, carry):
        q0 = pl.multiple_of(a * NA_GROUP, NA_GROUP)
        ga = i * (tq // NA_GROUP) + a
        variant = jnp.where(ga == 0, 1, jnp.where(ga == n_groups - 1, 2, 0))
        q = q_ref[0, 0, pl.ds(q0, NA_GROUP), :]
        k = kbuf[pl.ds(q0, NA_KEYS), :]
        v = vbuf[pl.ds(q0, NA_KEYS), :]
        s = _qk(q, k) * ATTN_SCALE + tab_ref[variant, 0]
        o, _ = _masked_softmax_pv(s, v)
        o_ref[0, 0, pl.ds(q0, NA_GROUP), :] = o.astype(o_ref.dtype)
        return carry

    lax.fori_loop(0, tq // NA_GROUP, group, 0)


def _na_attn(z, tab):
    b, _, l, _ = z.shape
    tq = min(NA_TQ, l)
    halo = NA_GROUP
    n_halo_blocks = l // halo
    r = tq // halo
    assert l % tq == 0 and l // NA_GROUP >= 3

    def cur(h0):
        return pl.BlockSpec((1, 1, tq, HEAD_DIM), lambda bi, h, i: (bi, h0 + h, i, 0))

    def prev(h0):
        return pl.BlockSpec((1, 1, halo, HEAD_DIM), lambda bi, h, i: (bi, h0 + h, jnp.maximum(i * r - 1, 0), 0))

    def nxt(h0):
        return pl.BlockSpec((1, 1, halo, HEAD_DIM),
                            lambda bi, h, i: (bi, h0 + h, jnp.minimum((i + 1) * r, n_halo_blocks - 1), 0))

    return pl.pallas_call(
        functools.partial(_na_body, n_groups=l // NA_GROUP),
        grid=(b, N_HEADS_NA, l // tq),
        in_specs=[cur(ZH_Q_NA), cur(ZH_K_NA), prev(ZH_K_NA), nxt(ZH_K_NA),
                  cur(ZH_V_NA), prev(ZH_V_NA), nxt(ZH_V_NA),
                  pl.BlockSpec((3, 1, NA_GROUP, NA_KEYS), lambda bi, h, i: (0, h, 0, 0))],
        out_specs=pl.BlockSpec((1, 1, tq, HEAD_DIM), lambda bi, h, i: (bi, h, i, 0)),
        out_shape=jax.ShapeDtypeStruct((b, N_HEADS_NA, l, HEAD_DIM), BF16),
        scratch_shapes=[pltpu.VMEM((tq + 2 * halo, HEAD_DIM), BF16)] * 2,
        compiler_params=_params("parallel", "parallel", "parallel"),
        name="na_attn",
    )(z, z, z, z, z, z, z, tab)


def _dil_bias_tables():
    h = np.arange(1, N_HEADS_DIL + 1, dtype=np.float32)
    slopes = np.power(np.float32(2.0), -ALIBI_MAX_BIAS * h / N_HEADS_DIL).astype(np.float32)
    rel = np.arange(DIL_K)[None, :] - (DIL_K - DIL_Q) // 2 - np.arange(DIL_Q)[:, None]
    tabs = []
    for g, (window, dilation) in enumerate(DIL_GROUPS):
        n = window // (2 * dilation)
        assert n == (DIL_K - DIL_Q) // 2
        dist = (np.abs(rel) * dilation).astype(np.float32)
        for jh in range(N_HEADS_PER_DIL):
            bias = -slopes[g * N_HEADS_PER_DIL + jh] * dist
            tabs.append(np.where(np.abs(rel) <= n, bias, np.float32(NEG_INF)))
    return jnp.asarray(np.stack(tabs).astype(np.float32))


def _dil_body(q0_ref, k0_ref, k0p_ref, k0n_ref, v0_ref, v0p_ref, v0n_ref,
              q1_ref, k1_ref, k1p_ref, k1n_ref, v1_ref, v1p_ref, v1n_ref,
              q2_ref, k2_ref, k2p_ref, k2n_ref, v2_ref, v2p_ref, v2n_ref,
              b0_ref, b1_ref, b2_ref, o_ref, qbuf, kbuf, vbuf, o_scr, l_scr, *, seq_len):
    i = pl.program_id(1)
    tq = q0_ref.shape[2]
    halo = (DIL_K - DIL_Q) // 2

    def unit(q, k, v, bias, key_lo, n_sub):
        kpos = key_lo + lax.broadcasted_iota(jnp.int32, (1, DIL_K), 1)
        s = _qk(q, k) * ATTN_SCALE + bias
        s = jnp.where((kpos >= 0) & (kpos < n_sub), s, NEG_INF)
        o, lse = _masked_softmax_pv(s, v)
        return o, jnp.broadcast_to(lse, (DIL_Q, HEAD_DIM))

    kbuf[0:halo, :] = k0p_ref[0, 0]
    kbuf[halo:halo + tq, :] = k0_ref[0, 0]
    kbuf[halo + tq:2 * halo + tq, :] = k0n_ref[0, 0]
    vbuf[0:halo, :] = v0p_ref[0, 0]
    vbuf[halo:halo + tq, :] = v0_ref[0, 0]
    vbuf[halo + tq:2 * halo + tq, :] = v0n_ref[0, 0]

    def unit0(c, carry):
        r0 = pl.multiple_of(c * DIL_Q, DIL_Q)
        o, lse = unit(q0_ref[0, 0, pl.ds(r0, DIL_Q), :], kbuf[pl.ds(r0, DIL_K), :], vbuf[pl.ds(r0, DIL_K), :],
                      b0_ref[0], i * tq + r0 - halo, seq_len)
        o_scr[0, pl.ds(r0, DIL_Q), :] = o
        l_scr[0, pl.ds(r0, DIL_Q), :] = lse
        return carry
    lax.fori_loop(0, tq // DIL_Q, unit0, 0)

    d = 4
    n_sb = tq // 256
    per_res_q = tq // d
    per_res_k = per_res_q + 2 * halo
    for e in range(d):
        for sb in range(n_sb):
            src = sb * 256 + e * 64
            qbuf[e * per_res_q + sb * 64:e * per_res_q + (sb + 1) * 64, :] = q1_ref[0, 0, src:src + 64, :]
            kbuf[e * per_res_k + halo + sb * 64:e * per_res_k + halo + (sb + 1) * 64, :] = k1_ref[0, 0, src:src + 64, :]
            vbuf[e * per_res_k + halo + sb * 64:e * per_res_k + halo + (sb + 1) * 64, :] = v1_ref[0, 0, src:src + 64, :]
        kbuf[e * per_res_k:e * per_res_k + halo, :] = k1p_ref[0, 0, e * 64:(e + 1) * 64, :]
        vbuf[e * per_res_k:e * per_res_k + halo, :] = v1p_ref[0, 0, e * 64:(e + 1) * 64, :]
        kbuf[e * per_res_k + halo + per_res_q:(e + 1) * per_res_k, :] = k1n_ref[0, 0, e * 64:(e + 1) * 64, :]
        vbuf[e * per_res_k + halo + per_res_q:(e + 1) * per_res_k, :] = v1n_ref[0, 0, e * 64:(e + 1) * 64, :]

    n_c1 = per_res_q // DIL_Q

    def unit1(t, carry):
        e = t // n_c1
        c = t % n_c1
        qo = pl.multiple_of(e * per_res_q + c * DIL_Q, DIL_Q)
        ko = pl.multiple_of(e * per_res_k + c * DIL_Q, DIL_Q)
        o, lse = unit(qbuf[pl.ds(qo, DIL_Q), :], kbuf[pl.ds(ko, DIL_K), :], vbuf[pl.ds(ko, DIL_K), :],
                      b1_ref[0], i * per_res_q + c * DIL_Q - halo, seq_len // d)
        t0 = c * (DIL_Q * d) + e
        o_scr[1, pl.ds(t0, DIL_Q, stride=d), :] = o
        l_scr[1, pl.ds(t0, DIL_Q, stride=d), :] = lse
        return carry
    lax.fori_loop(0, d * n_c1, unit1, 0)

    d = 16
    n_blk = tq // 1024
    per_res_q = tq // d
    per_res_k = per_res_q + 2 * halo
    for e in range(d):
        for sb in range(n_blk):
            src = sb * 1024 + e * 64
            qbuf[e * per_res_q + sb * 64:e * per_res_q + (sb + 1) * 64, :] = q2_ref[0, 0, src:src + 64, :]
            kbuf[e * per_res_k + halo + sb * 64:e * per_res_k + halo + (sb + 1) * 64, :] = k2_ref[0, 0, src:src + 64, :]
            vbuf[e * per_res_k + halo + sb * 64:e * per_res_k + halo + (sb + 1) * 64, :] = v2_ref[0, 0, src:src + 64, :]
        kbuf[e * per_res_k:e * per_res_k + halo, :] = k2p_ref[0, 0, e * 64:(e + 1) * 64, :]
        vbuf[e * per_res_k:e * per_res_k + halo, :] = v2p_ref[0, 0, e * 64:(e + 1) * 64, :]
        kbuf[e * per_res_k + halo + per_res_q:(e + 1) * per_res_k, :] = k2n_ref[0, 0, e * 64:(e + 1) * 64, :]
        vbuf[e * per_res_k + halo + per_res_q:(e + 1) * per_res_k, :] = v2n_ref[0, 0, e * 64:(e + 1) * 64, :]

    n_c2 = per_res_q // DIL_Q

    def unit2(t, carry):
        e = t // n_c2
        c = t % n_c2
        qo = pl.multiple_of(e * per_res_q + c * DIL_Q, DIL_Q)
        ko = pl.multiple_of(e * per_res_k + c * DIL_Q, DIL_Q)
        o, lse = unit(qbuf[pl.ds(qo, DIL_Q), :], kbuf[pl.ds(ko, DIL_K), :], vbuf[pl.ds(ko, DIL_K), :],
                      b2_ref[0], i * per_res_q + c * DIL_Q - halo, seq_len // d)
        t0 = c * (DIL_Q * d) + e
        o_scr[2, pl.ds(t0, DIL_Q, stride=d), :] = o
        l_scr[2, pl.ds(t0, DIL_Q, stride=d), :] = lse
        return carry
    lax.fori_loop(0, d * n_c2, unit2, 0)

    rows = 256

    def merge(r, carry):
        r0 = pl.multiple_of(r * rows, rows)
        l0 = l_scr[0, pl.ds(r0, rows), :]
        l1 = l_scr[1, pl.ds(r0, rows), :]
        l2 = l_scr[2, pl.ds(r0, rows), :]
        m = jnp.maximum(jnp.maximum(l0, l1), l2)
        w0, w1, w2 = jnp.exp(l0 - m), jnp.exp(l1 - m), jnp.exp(l2 - m)
        den = w0 + w1 + w2
        num = (w0 / den) * o_scr[0, pl.ds(r0, rows), :] + (w1 / den) * o_scr[1, pl.ds(r0, rows), :] \
            + (w2 / den) * o_scr[2, pl.ds(r0, rows), :]
        o_ref[0, 0, pl.ds(r0, rows), :] = num.astype(o_ref.dtype)
        return carry
    lax.fori_loop(0, tq // rows, merge, 0)


def _dil_attn(z, tab):
    b, _, l, _ = z.shape
    tq = min(DIL_TQ, l)
    halo = (DIL_K - DIL_Q) // 2
    assert l % tq == 0 and tq % 2048 == 0

    def cur(h0):
        return pl.BlockSpec((1, 1, tq, HEAD_DIM), lambda bi, i, h: (bi, h0 + h, i, 0))

    def prev(h0, rows):
        r = tq // rows
        return pl.BlockSpec((1, 1, rows, HEAD_DIM), lambda bi, i, h: (bi, h0 + h, jnp.maximum(i * r - 1, 0), 0))

    def nxt(h0, rows):
        r = tq // rows
        nblk = l // rows
        return pl.BlockSpec((1, 1, rows, HEAD_DIM),
                            lambda bi, i, h: (bi, h0 + h, jnp.minimum((i + 1) * r, nblk - 1), 0))

    in_specs = []
    for g, (_, dilation) in enumerate(DIL_GROUPS):
        hrows = halo * dilation
        in_specs += [cur(ZH_Q_DL[g]),
                     cur(ZH_K_DL[g]), prev(ZH_K_DL[g], hrows), nxt(ZH_K_DL[g], hrows),
                     cur(ZH_V_DL[g]), prev(ZH_V_DL[g], hrows), nxt(ZH_V_DL[g], hrows)]
    for g in range(len(DIL_GROUPS)):
        in_specs.append(pl.BlockSpec((1, DIL_Q, DIL_K), lambda bi, i, h, g=g: (g * N_HEADS_PER_DIL + h, 0, 0)))

    buf_rows = 16 * (tq // 16 + 2 * halo)
    return pl.pallas_call(
        functools.partial(_dil_body, seq_len=l),
        grid=(b, l // tq, N_HEADS_PER_DIL),
        in_specs=in_specs,
        out_specs=pl.BlockSpec((1, 1, tq, HEAD_DIM), lambda bi, i, h: (bi, h, i, 0)),
        out_shape=jax.ShapeDtypeStruct((b, N_HEADS_PER_DIL, l, HEAD_DIM), BF16),
        scratch_shapes=[pltpu.VMEM((tq, HEAD_DIM), BF16),
                        pltpu.VMEM((buf_rows, HEAD_DIM), BF16),
                        pltpu.VMEM((buf_rows, HEAD_DIM), BF16),
                        pltpu.VMEM((3, tq, HEAD_DIM), F32),
                        pltpu.VMEM((3, tq, HEAD_DIM), F32)],
        compiler_params=_params("parallel", "parallel", "parallel"),
        name="dil_attn",
    )(*([z] * 21), tab, tab, tab)


def _out_proj_body(ona_ref, odl_ref, gna_ref, gdl_ref, x_ref, gt_ref, wna_ref, wdl_ref, wo_ref, o_ref):
    def heads(ref):
        return jnp.concatenate([ref[0, h] for h in range(ref.shape[1])], axis=-1)

    y_na = jnp.dot(heads(ona_ref), wna_ref[...], preferred_element_type=F32)
    y_dl = jnp.dot(heads(odl_ref), wdl_ref[...], preferred_element_type=F32)
    merged = jax.nn.sigmoid(heads(gna_ref).astype(F32)) * y_na + jax.nn.sigmoid(heads(gdl_ref).astype(F32)) * y_dl
    mix = jnp.dot(merged.astype(BF16), wo_ref[...], preferred_element_type=F32)
    o_ref[0] = x_ref[0] + gt_ref[0] * mix


def _out_proj(o_na, o_dl, z, x, gt, w_na_out, w_dil_out, w_o):
    b, l, d = x.shape
    tm = min(OUT_TM, l)
    n_gate_heads = d // HEAD_DIM
    assert ZH_G_NA == 0 and ZH_G_DL == n_gate_heads

    def const(shape):
        return pl.BlockSpec(shape, lambda bi, i: (0, 0), pipeline_mode=pl.Buffered(1))

    return pl.pallas_call(
        _out_proj_body,
        grid=(b, l // tm),
        in_specs=[pl.BlockSpec((1, N_HEADS_NA, tm, HEAD_DIM), lambda bi, i: (bi, 0, i, 0)),
                  pl.BlockSpec((1, N_HEADS_PER_DIL, tm, HEAD_DIM), lambda bi, i: (bi, 0, i, 0)),
                  pl.BlockSpec((1, n_gate_heads, tm, HEAD_DIM), lambda bi, i: (bi, 0, i, 0)),
                  pl.BlockSpec((1, n_gate_heads, tm, HEAD_DIM), lambda bi, i: (bi, 1, i, 0)),
                  pl.BlockSpec((1, tm, d), lambda bi, i: (bi, i, 0)),
                  pl.BlockSpec((1, 1, d), lambda bi, i: (bi, 0, 0)),
                  const(w_na_out.shape), const(w_dil_out.shape), const(w_o.shape)],
        out_specs=pl.BlockSpec((1, tm, d), lambda bi, i: (bi, i, 0)),
        out_shape=jax.ShapeDtypeStruct((b, l, d), F32),
        compiler_params=_params("parallel", "parallel"),
        name="out_proj",
    )(o_na, o_dl, z, z, x, gt, w_na_out, w_dil_out, w_o)


def _ffn_body(x_ref, sh_ref, sc_ref, gt_ref, g_ref, gf_ref, w1_ref, w2_ref, o_ref, u_ref):
    f = pl.program_id(2)

    @pl.when(f == 0)
    def _():
        u_ref[...] = _modulated_norm(x_ref[0], g_ref[...], sc_ref[0], sh_ref[0]).astype(BF16)

    h = jnp.dot(u_ref[...], w1_ref[...], preferred_element_type=F32)
    h = jnp.square(jnp.maximum(h, 0.0)).astype(BF16)
    part = jnp.dot(h, w2_ref[...], preferred_element_type=F32)

    @pl.when(f == 0)
    def _():
        o_ref[0] = part

    @pl.when(f > 0)
    def _():
        o_ref[0] += part

    @pl.when(f == pl.num_programs(2) - 1)
    def _():
        x2 = x_ref[0] + gt_ref[0] * o_ref[0]
        ms = jnp.mean(x2 * x2, axis=-1, keepdims=True)
        o_ref[0] = x2 * lax.rsqrt(ms + NORM_EPS) * gf_ref[...]


def _ffn(x, sh, sc, gt, g_ffn, g_final, w_ff1, w_ff2):
    b, l, d = x.shape
    tm = min(FFN_TM, l)
    d_ff = w_ff1.shape[1]
    tf = FFN_TF
    return pl.pallas_call(
        _ffn_body,
        grid=(b, l // tm, d_ff // tf),
        in_specs=[pl.BlockSpec((1, tm, d), lambda bi, i, f: (bi, i, 0)),
                  pl.BlockSpec((1, 1, d), lambda bi, i, f: (bi, 0, 0)),
                  pl.BlockSpec((1, 1, d), lambda bi, i, f: (bi, 0, 0)),
                  pl.BlockSpec((1, 1, d), lambda bi, i, f: (bi, 0, 0)),
                  pl.BlockSpec((1, d), lambda bi, i, f: (0, 0)),
                  pl.BlockSpec((1, d), lambda bi, i, f: (0, 0)),
                  pl.BlockSpec((d, tf), lambda bi, i, f: (0, f)),
                  pl.BlockSpec((tf, d), lambda bi, i, f: (f, 0))],
        out_specs=pl.BlockSpec((1, tm, d), lambda bi, i, f: (bi, i, 0)),
        out_shape=jax.ShapeDtypeStruct((b, l, d), F32),
        scratch_shapes=[pltpu.VMEM((tm, d), BF16)],
        compiler_params=_params("parallel", "parallel", "arbitrary"),
        name="ffn",
    )(x, sh, sc, gt, g_ffn, g_final, w_ff1, w_ff2)


def _layer(x, mod, g_mix, w_in_r, na_tab, dil_tab, w_na_out, w_dil_out, w_o, g_ffn, w_ff1, w_ff2, g_final):
    d = x.shape[-1]
    sh1, sc1, gt1, sh2, sc2, gt2 = [mod[:, None, k * d:(k + 1) * d] for k in range(6)]
    z = _in_proj(x, sh1, sc1, g_mix, w_in_r)
    o_na = _na_attn(z, na_tab)
    o_dl = _dil_attn(z, dil_tab)
    x1 = _out_proj(o_na, o_dl, z, x, gt1, w_na_out, w_dil_out, w_o)
    return _ffn(x1, sh2, sc2, gt2, g_ffn, g_final, w_ff1, w_ff2)


def kernel(x_prompt, x_sample, c_prompt, c_sample, w_mod, b_mod, g_mix, w_in, rpb, w_na_out, w_dil_out, w_o,
           g_ffn, w_ff1, w_ff2, g_final):
    assert w_mod.shape[0] == 1, "single-layer trunk"
    d = x_prompt.shape[-1]
    nb_p, nb_s = c_prompt.shape[0], c_sample.shape[0]
    pad = (-(nb_p + nb_s)) % 8
    c_all = jnp.concatenate([c_prompt, c_sample, jnp.zeros((pad, d), F32)], axis=0)
    mod = _modulation(c_all, w_mod[0], b_mod[0])

    w_in_r = jnp.concatenate([w_in[0][:, c0:c0 + n] for c0, n in _w_in_column_order(d)], axis=1).astype(BF16)
    shared = (g_mix[0][None], w_in_r, _na_bias_tables(rpb[0]), _dil_bias_tables(),
              w_na_out[0].astype(BF16), w_dil_out[0].astype(BF16), w_o[0].astype(BF16),
              g_ffn[0][None], w_ff1[0].astype(BF16), w_ff2[0].astype(BF16), g_final[None])
    y_prompt = _layer(x_prompt, mod[:nb_p], *shared)
    y_sample = _layer(x_sample, mod[nb_p:nb_p + nb_s], *shared)
    return (y_prompt, y_sample)
```

```python
import functools

import numpy as np
import jax
import jax.numpy as jnp
from jax import lax
from jax.experimental import pallas as pl
from jax.experimental.pallas import tpu as pltpu

F32 = jnp.float32
BF16 = jnp.bfloat16

HEAD_DIM = 128
N_HEADS_NA = 4
NA_ROWS = 8
NA_KW = 16
GRID_W = 64
DIL_GROUPS = ((128, 1), (512, 4), (2048, 16))
N_HEADS_PER_DIL = 4
N_HEADS_DIL = N_HEADS_PER_DIL * len(DIL_GROUPS)
ALIBI_MAX_BIAS = 8.0
NORM_EPS = 1e-6
NEG_INF = -1e30
ATTN_SCALE = HEAD_DIM ** -0.5

VMEM_LIMIT_BYTES = 60 * 1024 * 1024

ZH_G_NA, ZH_G_DL = 0, 16
ZH_Q_NA, ZH_K_NA, ZH_V_NA = 32, 36, 40
ZH_Q_DL = (44, 56, 68)
ZH_K_DL = (48, 60, 72)
ZH_V_DL = (52, 64, 76)
Z_HEADS = 80

IN_TM = 1024
IN_TN = 2048
IN_TILE_KINDS = ((0, 0, 0, 0), (0, 0, 0, 0), (0, 0, 0, 0), (0, 0, 1, 1), (1, 2, 2, 2))

NA_TQ = 1024
NA_GROUP = 4 * GRID_W
NA_KEYS = 12 * GRID_W
DIL_TQ = 2048
DIL_Q = 128
DIL_K = 256
DIL_UNROLL = 16
NA_UNROLL = 4
OUT_TM = 512
FFN_TM = 512
FFN_TF = 1024


def _params(*sem):
    return pltpu.CompilerParams(dimension_semantics=sem, vmem_limit_bytes=VMEM_LIMIT_BYTES)


def _w_in_column_order(d_model):
    d_na = N_HEADS_NA * HEAD_DIM
    d_dil = N_HEADS_DIL * HEAD_DIM
    d_grp = N_HEADS_PER_DIL * HEAD_DIM
    q_na, k_na, v_na = 0, d_na, 2 * d_na
    q_dl, k_dl, v_dl = 3 * d_na, 3 * d_na + d_dil, 3 * d_na + 2 * d_dil
    g_na = 3 * d_na + 3 * d_dil
    g_dl = g_na + d_model
    order = [(g_na, d_model), (g_dl, d_model), (q_na, d_na), (k_na, d_na), (v_na, d_na)]
    for g in range(len(DIL_GROUPS)):
        order += [(q_dl + g * d_grp, d_grp), (k_dl + g * d_grp, d_grp), (v_dl + g * d_grp, d_grp)]
    return order


def _mod_body(c_ref, w_ref, b_ref, o_ref):
    c = c_ref[...]
    s = (c * jax.nn.sigmoid(c)).astype(BF16)
    o_ref[...] = jnp.dot(s, w_ref[...].astype(BF16), preferred_element_type=F32) + b_ref[...]


def _modulation(c, w_mod, b_mod):
    rows, d = c.shape
    n = w_mod.shape[1]
    tn = 1024
    return pl.pallas_call(
        _mod_body,
        grid=(n // tn,),
        in_specs=[pl.BlockSpec((rows, d), lambda j: (0, 0)),
                  pl.BlockSpec((d, tn), lambda j: (0, j)),
                  pl.BlockSpec((1, tn), lambda j: (0, j))],
        out_specs=pl.BlockSpec((rows, tn), lambda j: (0, j)),
        out_shape=jax.ShapeDtypeStruct((rows, n), F32),
        compiler_params=_params("parallel"),
        name="modulation",
    )(c, w_mod, b_mod.reshape(1, n))


def _modulated_norm(x, g, sc, sh):
    ms = jnp.mean(x * x, axis=-1, keepdims=True)
    y = x * lax.rsqrt(ms + NORM_EPS) * g
    return y * (1.0 + sc) + sh


def _masked_softmax_pv(s, v):
    m = jnp.max(s, axis=-1, keepdims=True)
    p = jnp.exp(s - m)
    l = jnp.sum(p, axis=-1, keepdims=True)
    o = jnp.dot(p.astype(BF16), v, preferred_element_type=F32) / l
    return o, m + jnp.log(l)


def _qk(q, k):
    return lax.dot_general(q, k, (((1,), (1,)), ((), ())), preferred_element_type=F32)


def _in_proj_body(x_ref, sh_ref, sc_ref, g_ref, w_ref, z_ref, u_ref, p_ref):
    j = pl.program_id(2)
    tm = u_ref.shape[0]
    rows = 64

    @pl.when(j == 0)
    def _():
        def chunk(r, carry):
            r0 = pl.multiple_of(r * rows, rows)
            u = _modulated_norm(x_ref[0, pl.ds(r0, rows), :], g_ref[...], sc_ref[0], sh_ref[0])
            u_ref[pl.ds(r0, rows), :] = u.astype(BF16)
            return carry
        lax.fori_loop(0, tm // rows, chunk, 0)

    def subtile(s, kind):
        acc = jnp.dot(u_ref[...], w_ref[:, s * 512:(s + 1) * 512], preferred_element_type=F32)
        for hh in range(4):
            a = acc[:, hh * HEAD_DIM:(hh + 1) * HEAD_DIM]
            zh = s * 4 + hh
            if kind == 0:
                z_ref[0, zh] = a.astype(BF16)
                continue
            p_ref[hh] = a
            if kind == 1:
                for sb in range(tm // 256):
                    for e in range(4):
                        r0 = sb * 256 + e * 64
                        z_ref[0, zh, r0:r0 + 64, :] = p_ref[hh, pl.ds(sb * 256 + e, 64, stride=4), :].astype(BF16)
            else:
                for sb in range(tm // 1024):
                    for e in range(16):
                        r0 = sb * 1024 + e * 64
                        z_ref[0, zh, r0:r0 + 64, :] = p_ref[hh, pl.ds(sb * 1024 + e, 64, stride=16), :].astype(BF16)

    @pl.when(j < 3)
    def _():
        for s in range(4):
            subtile(s, 0)

    for jj in (3, 4):
        @pl.when(j == jj)
        def _(jj=jj):
            for s in range(4):
                subtile(s, IN_TILE_KINDS[jj][s])


def _in_proj(x, sh, sc, g, w_in_r):
    b, l, d = x.shape
    tm = min(IN_TM, l)
    n_tiles = w_in_r.shape[1] // IN_TN
    assert n_tiles == len(IN_TILE_KINDS) and l % tm == 0 and tm % 1024 == 0
    return pl.pallas_call(
        _in_proj_body,
        grid=(b, l // tm, n_tiles),
        in_specs=[pl.BlockSpec((1, tm, d), lambda bi, i, j: (bi, i, 0)),
                  pl.BlockSpec((1, 1, d), lambda bi, i, j: (bi, 0, 0)),
                  pl.BlockSpec((1, 1, d), lambda bi, i, j: (bi, 0, 0)),
                  pl.BlockSpec((1, d), lambda bi, i, j: (0, 0)),
                  pl.BlockSpec((d, IN_TN), lambda bi, i, j: (0, j))],
        out_specs=pl.BlockSpec((1, IN_TN // HEAD_DIM, tm, HEAD_DIM), lambda bi, i, j: (bi, j, i, 0)),
        out_shape=jax.ShapeDtypeStruct((b, Z_HEADS, l, HEAD_DIM), BF16),
        scratch_shapes=[pltpu.VMEM((tm, d), BF16), pltpu.VMEM((4, tm, HEAD_DIM), F32)],
        compiler_params=_params("parallel", "parallel", "arbitrary"),
        name="in_proj",
    )(x, sh, sc, g, w_in_r)


def _na_bias_tables(rpb):
    rho = np.arange(4)[:, None, None, None]
    qc = np.arange(GRID_W)[None, :, None, None]
    f = np.arange(12)[None, None, :, None]
    kc = np.arange(GRID_W)[None, None, None, :]
    col_start = np.clip(qc - NA_KW // 2, 0, GRID_W - NA_KW)
    col_ok = (kc >= col_start) & (kc < col_start + NA_KW)
    row_ok = np.stack([
        (f >= rho) & (f < rho + NA_ROWS),
        (f >= 4) & (f < 4 + NA_ROWS) & (rho >= 0),
        (f >= 0) & (f < NA_ROWS) & (rho >= 0),
    ])
    ok = row_ok & col_ok[None]
    padded = jnp.pad(rpb.astype(F32), ((0, 0), (0, 0), (GRID_W, GRID_W)))
    c0 = GRID_W + NA_KW - 1
    col = jnp.stack([padded[:, :, c0 - q:c0 - q + GRID_W] for q in range(GRID_W)], axis=2)
    bias = jnp.stack([jnp.stack([col[:, fr - r + 3] for fr in range(12)], axis=2) for r in range(4)], axis=1)
    tab = jnp.where(ok[:, None], bias[None], NEG_INF)
    return tab.reshape(3, rpb.shape[0], NA_GROUP, NA_KEYS)


def _na_body(q_ref, k_ref, kp_ref, kn_ref, v_ref, vp_ref, vn_ref, tab_ref, o_ref, kbuf, vbuf, *, n_groups):
    i = pl.program_id(2)
    tq = q_ref.shape[2]
    halo = kp_ref.shape[2]
    kbuf[0:halo, :] = kp_ref[0, 0]
    kbuf[halo:halo + tq, :] = k_ref[0, 0]
    kbuf[halo + tq:, :] = kn_ref[0, 0]
    vbuf[0:halo, :] = vp_ref[0, 0]
    vbuf[halo:halo + tq, :] = v_ref[0, 0]
    vbuf[halo + tq:, :] = vn_ref[0, 0]

    def group(a, carry):
        q0 = pl.multiple_of(a * NA_GROUP, NA_GROUP)
        ga = i * (tq // NA_GROUP) + a
        variant = jnp.where(ga == 0, 1, jnp.where(ga == n_groups - 1, 2, 0))
        q = q_ref[0, 0, pl.ds(q0, NA_GROUP), :]
        k = kbuf[pl.ds(q0, NA_KEYS), :]
        v = vbuf[pl.ds(q0, NA_KEYS), :]
        s = _qk(q, k) * ATTN_SCALE + tab_ref[variant, 0]
        o, _ = _masked_softmax_pv(s, v)
        o_ref[0, 0, pl.ds(q0, NA_GROUP), :] = o.astype(o_ref.dtype)
        return carry

    lax.fori_loop(0, tq // NA_GROUP, group, 0, unroll=NA_UNROLL)


def _na_attn(z, tab):
    b, _, l, _ = z.shape
    tq = min(NA_TQ, l)
    halo = NA_GROUP
    n_halo_blocks = l // halo
    r = tq // halo
    assert l % tq == 0 and l // NA_GROUP >= 3

    def cur(h0):
        return pl.BlockSpec((1, 1, tq, HEAD_DIM), lambda bi, h, i: (bi, h0 + h, i, 0))

    def prev(h0):
        return pl.BlockSpec((1, 1, halo, HEAD_DIM), lambda bi, h, i: (bi, h0 + h, jnp.maximum(i * r - 1, 0), 0))

    def nxt(h0):
        return pl.BlockSpec((1, 1, halo, HEAD_DIM),
                            lambda bi, h, i: (bi, h0 + h, jnp.minimum((i + 1) * r, n_halo_blocks - 1), 0))

    return pl.pallas_call(
        functools.partial(_na_body, n_groups=l // NA_GROUP),
        grid=(b, N_HEADS_NA, l // tq),
        in_specs=[cur(ZH_Q_NA), cur(ZH_K_NA), prev(ZH_K_NA), nxt(ZH_K_NA),
                  cur(ZH_V_NA), prev(ZH_V_NA), nxt(ZH_V_NA),
                  pl.BlockSpec((3, 1, NA_GROUP, NA_KEYS), lambda bi, h, i: (0, h, 0, 0))],
        out_specs=pl.BlockSpec((1, 1, tq, HEAD_DIM), lambda bi, h, i: (bi, h, i, 0)),
        out_shape=jax.ShapeDtypeStruct((b, N_HEADS_NA, l, HEAD_DIM), BF16),
        scratch_shapes=[pltpu.VMEM((tq + 2 * halo, HEAD_DIM), BF16)] * 2,
        compiler_params=_params("parallel", "parallel", "parallel"),
        name="na_attn",
    )(z, z, z, z, z, z, z, tab)


def _dil_bias_tables():
    h = np.arange(1, N_HEADS_DIL + 1, dtype=np.float32)
    slopes = np.power(np.float32(2.0), -ALIBI_MAX_BIAS * h / N_HEADS_DIL).astype(np.float32)
    rel = np.arange(DIL_K)[None, :] - (DIL_K - DIL_Q) // 2 - np.arange(DIL_Q)[:, None]
    tabs = []
    for g, (window, dilation) in enumerate(DIL_GROUPS):
        n = window // (2 * dilation)
        assert n == (DIL_K - DIL_Q) // 2
        dist = (np.abs(rel) * dilation).astype(np.float32)
        for jh in range(N_HEADS_PER_DIL):
            bias = -slopes[g * N_HEADS_PER_DIL + jh] * dist
            tabs.append(np.where(np.abs(rel) <= n, bias, np.float32(NEG_INF)))
    return jnp.asarray(np.stack(tabs).astype(np.float32))


def _dil_body(q0_ref, k0_ref, k0p_ref, k0n_ref, v0_ref, v0p_ref, v0n_ref,
              q1_ref, k1_ref, k1p_ref, k1n_ref, v1_ref, v1p_ref, v1n_ref,
              q2_ref, k2_ref, k2p_ref, k2n_ref, v2_ref, v2p_ref, v2n_ref,
              b0_ref, b1_ref, b2_ref, o_ref, qbuf, kbuf, vbuf, o_scr, l_scr, *, seq_len):
    i = pl.program_id(1)
    tq = q0_ref.shape[2]
    halo = (DIL_K - DIL_Q) // 2

    def unit(q, k, v, bias, key_lo, n_sub):
        kpos = key_lo + lax.broadcasted_iota(jnp.int32, (1, DIL_K), 1)
        s = _qk(q, k) * ATTN_SCALE + bias
        s = jnp.where((kpos >= 0) & (kpos < n_sub), s, NEG_INF)
        o, lse = _masked_softmax_pv(s, v)
        return o, jnp.broadcast_to(lse, (DIL_Q, HEAD_DIM))

    kbuf[0:halo, :] = k0p_ref[0, 0]
    kbuf[halo:halo + tq, :] = k0_ref[0, 0]
    kbuf[halo + tq:2 * halo + tq, :] = k0n_ref[0, 0]
    vbuf[0:halo, :] = v0p_ref[0, 0]
    vbuf[halo:halo + tq, :] = v0_ref[0, 0]
    vbuf[halo + tq:2 * halo + tq, :] = v0n_ref[0, 0]

    def unit0(c, carry):
        r0 = pl.multiple_of(c * DIL_Q, DIL_Q)
        o, lse = unit(q0_ref[0, 0, pl.ds(r0, DIL_Q), :], kbuf[pl.ds(r0, DIL_K), :], vbuf[pl.ds(r0, DIL_K), :],
                      b0_ref[0], i * tq + r0 - halo, seq_len)
        o_scr[0, pl.ds(r0, DIL_Q), :] = o
        l_scr[0, pl.ds(r0, DIL_Q), :] = lse
        return carry
    lax.fori_loop(0, tq // DIL_Q, unit0, 0, unroll=DIL_UNROLL)

    d = 4
    n_sb = tq // 256
    per_res_q = tq // d
    per_res_k = per_res_q + 2 * halo
    for e in range(d):
        for sb in range(n_sb):
            src = sb * 256 + e * 64
            qbuf[e * per_res_q + sb * 64:e * per_res_q + (sb + 1) * 64, :] = q1_ref[0, 0, src:src + 64, :]
            kbuf[e * per_res_k + halo + sb * 64:e * per_res_k + halo + (sb + 1) * 64, :] = k1_ref[0, 0, src:src + 64, :]
            vbuf[e * per_res_k + halo + sb * 64:e * per_res_k + halo + (sb + 1) * 64, :] = v1_ref[0, 0, src:src + 64, :]
        kbuf[e * per_res_k:e * per_res_k + halo, :] = k1p_ref[0, 0, e * 64:(e + 1) * 64, :]
        vbuf[e * per_res_k:e * per_res_k + halo, :] = v1p_ref[0, 0, e * 64:(e + 1) * 64, :]
        kbuf[e * per_res_k + halo + per_res_q:(e + 1) * per_res_k, :] = k1n_ref[0, 0, e * 64:(e + 1) * 64, :]
        vbuf[e * per_res_k + halo + per_res_q:(e + 1) * per_res_k, :] = v1n_ref[0, 0, e * 64:(e + 1) * 64, :]

    n_c1 = per_res_q // DIL_Q

    def unit1(t, carry):
        e = t // n_c1
        c = t % n_c1
        qo = pl.multiple_of(e * per_res_q + c * DIL_Q, DIL_Q)
        ko = pl.multiple_of(e * per_res_k + c * DIL_Q, DIL_Q)
        o, lse = unit(qbuf[pl.ds(qo, DIL_Q), :], kbuf[pl.ds(ko, DIL_K), :], vbuf[pl.ds(ko, DIL_K), :],
                      b1_ref[0], i * per_res_q + c * DIL_Q - halo, seq_len // d)
        t0 = c * (DIL_Q * d) + e
        o_scr[1, pl.ds(t0, DIL_Q, stride=d), :] = o
        l_scr[1, pl.ds(t0, DIL_Q, stride=d), :] = lse
        return carry
    lax.fori_loop(0, d * n_c1, unit1, 0, unroll=DIL_UNROLL)

    d = 16
    n_blk = tq // 1024
    per_res_q = tq // d
    per_res_k = per_res_q + 2 * halo
    for e in range(d):
        for sb in range(n_blk):
            src = sb * 1024 + e * 64
            qbuf[e * per_res_q + sb * 64:e * per_res_q + (sb + 1) * 64, :] = q2_ref[0, 0, src:src + 64, :]
            kbuf[e * per_res_k + halo + sb * 64:e * per_res_k + halo + (sb + 1) * 64, :] = k2_ref[0, 0, src:src + 64, :]
            vbuf[e * per_res_k + halo + sb * 64:e * per_res_k + halo + (sb + 1) * 64, :] = v2_ref[0, 0, src:src + 64, :]
        kbuf[e * per_res_k:e * per_res_k + halo, :] = k2p_ref[0, 0, e * 64:(e + 1) * 64, :]
        vbuf[e * per_res_k:e * per_res_k + halo, :] = v2p_ref[0, 0, e * 64:(e + 1) * 64, :]
        kbuf[e * per_res_k + halo + per_res_q:(e + 1) * per_res_k, :] = k2n_ref[0, 0, e * 64:(e + 1) * 64, :]
        vbuf[e * per_res_k + halo + per_res_q:(e + 1) * per_res_k, :] = v2n_ref[0, 0, e * 64:(e + 1) * 64, :]

    n_c2 = per_res_q // DIL_Q

    def unit2(t, carry):
        e = t // n_c2
        c = t % n_c2
        qo = pl.multiple_of(e * per_res_q + c * DIL_Q, DIL_Q)
        ko = pl.multiple_of(e * per_res_k + c * DIL_Q, DIL_Q)
        o, lse = unit(qbuf[pl.ds(qo, DIL_Q), :], kbuf[pl.ds(ko, DIL_K), :], vbuf[pl.ds(ko, DIL_K), :],
                      b2_ref[0], i * per_res_q + c * DIL_Q - halo, seq_len // d)
        t0 = c * (DIL_Q * d) + e
        o_scr[2, pl.ds(t0, DIL_Q, stride=d), :] = o
        l_scr[2, pl.ds(t0, DIL_Q, stride=d), :] = lse
        return carry
    lax.fori_loop(0, d * n_c2, unit2, 0, unroll=DIL_UNROLL)

    rows = 256

    def merge(r, carry):
        r0 = pl.multiple_of(r * rows, rows)
        l0 = l_scr[0, pl.ds(r0, rows), :]
        l1 = l_scr[1, pl.ds(r0, rows), :]
        l2 = l_scr[2, pl.ds(r0, rows), :]
        m = jnp.maximum(jnp.maximum(l0, l1), l2)
        w0, w1, w2 = jnp.exp(l0 - m), jnp.exp(l1 - m), jnp.exp(l2 - m)
        den = w0 + w1 + w2
        num = (w0 / den) * o_scr[0, pl.ds(r0, rows), :] + (w1 / den) * o_scr[1, pl.ds(r0, rows), :] \
            + (w2 / den) * o_scr[2, pl.ds(r0, rows), :]
        o_ref[0, 0, pl.ds(r0, rows), :] = num.astype(o_ref.dtype)
        return carry
    lax.fori_loop(0, tq // rows, merge, 0)


def _dil_attn(z, tab):
    b, _, l, _ = z.shape
    tq = min(DIL_TQ, l)
    halo = (DIL_K - DIL_Q) // 2
    assert l % tq == 0 and tq % 2048 == 0

    def cur(h0):
        return pl.BlockSpec((1, 1, tq, HEAD_DIM), lambda bi, i, h: (bi, h0 + h, i, 0))

    def prev(h0, rows):
        r = tq // rows
        return pl.BlockSpec((1, 1, rows, HEAD_DIM), lambda bi, i, h: (bi, h0 + h, jnp.maximum(i * r - 1, 0), 0))

    def nxt(h0, rows):
        r = tq // rows
        nblk = l // rows
        return pl.BlockSpec((1, 1, rows, HEAD_DIM),
                            lambda bi, i, h: (bi, h0 + h, jnp.minimum((i + 1) * r, nblk - 1), 0))

    in_specs = []
    for g, (_, dilation) in enumerate(DIL_GROUPS):
        hrows = halo * dilation
        in_specs += [cur(ZH_Q_DL[g]),
                     cur(ZH_K_DL[g]), prev(ZH_K_DL[g], hrows), nxt(ZH_K_DL[g], hrows),
                     cur(ZH_V_DL[g]), prev(ZH_V_DL[g], hrows), nxt(ZH_V_DL[g], hrows)]
    for g in range(len(DIL_GROUPS)):
        in_specs.append(pl.BlockSpec((1, DIL_Q, DIL_K), lambda bi, i, h, g=g: (g * N_HEADS_PER_DIL + h, 0, 0)))

    buf_rows = 16 * (tq // 16 + 2 * halo)
    return pl.pallas_call(
        functools.partial(_dil_body, seq_len=l),
        grid=(b, l // tq, N_HEADS_PER_DIL),
        in_specs=in_specs,
        out_specs=pl.BlockSpec((1, 1, tq, HEAD_DIM), lambda bi, i, h: (bi, h, i, 0)),
        out_shape=jax.ShapeDtypeStruct((b, N_HEADS_PER_DIL, l, HEAD_DIM), BF16),
        scratch_shapes=[pltpu.VMEM((tq, HEAD_DIM), BF16),
                        pltpu.VMEM((buf_rows, HEAD_DIM), BF16),
                        pltpu.VMEM((buf_rows, HEAD_DIM), BF16),
                        pltpu.VMEM((3, tq, HEAD_DIM), F32),
                        pltpu.VMEM((3, tq, HEAD_DIM), F32)],
        compiler_params=_params("parallel", "parallel", "parallel"),
        name="dil_attn",
    )(*([z] * 21), tab, tab, tab)


def _out_proj_body(ona_ref, odl_ref, gna_ref, gdl_ref, x_ref, gt_ref, wna_ref, wdl_ref, wo_ref, o_ref):
    def heads(ref):
        return jnp.concatenate([ref[0, h] for h in range(ref.shape[1])], axis=-1)

    y_na = jnp.dot(heads(ona_ref), wna_ref[...], preferred_element_type=F32)
    y_dl = jnp.dot(heads(odl_ref), wdl_ref[...], preferred_element_type=F32)
    merged = jax.nn.sigmoid(heads(gna_ref).astype(F32)) * y_na + jax.nn.sigmoid(heads(gdl_ref).astype(F32)) * y_dl
    mix = jnp.dot(merged.astype(BF16), wo_ref[...], preferred_element_type=F32)
    o_ref[0] = x_ref[0] + gt_ref[0] * mix


def _out_proj(o_na, o_dl, z, x, gt, w_na_out, w_dil_out, w_o):
    b, l, d = x.shape
    tm = min(OUT_TM, l)
    n_gate_heads = d // HEAD_DIM
    assert ZH_G_NA == 0 and ZH_G_DL == n_gate_heads

    def const(shape):
        return pl.BlockSpec(shape, lambda bi, i: (0, 0), pipeline_mode=pl.Buffered(1))

    return pl.pallas_call(
        _out_proj_body,
        grid=(b, l // tm),
        in_specs=[pl.BlockSpec((1, N_HEADS_NA, tm, HEAD_DIM), lambda bi, i: (bi, 0, i, 0)),
                  pl.BlockSpec((1, N_HEADS_PER_DIL, tm, HEAD_DIM), lambda bi, i: (bi, 0, i, 0)),
                  pl.BlockSpec((1, n_gate_heads, tm, HEAD_DIM), lambda bi, i: (bi, 0, i, 0)),
                  pl.BlockSpec((1, n_gate_heads, tm, HEAD_DIM), lambda bi, i: (bi, 1, i, 0)),
                  pl.BlockSpec((1, tm, d), lambda bi, i: (bi, i, 0)),
                  pl.BlockSpec((1, 1, d), lambda bi, i: (bi, 0, 0)),
                  const(w_na_out.shape), const(w_dil_out.shape), const(w_o.shape)],
        out_specs=pl.BlockSpec((1, tm, d), lambda bi, i: (bi, i, 0)),
        out_shape=jax.ShapeDtypeStruct((b, l, d), F32),
        compiler_params=_params("parallel", "parallel"),
        name="out_proj",
    )(o_na, o_dl, z, z, x, gt, w_na_out, w_dil_out, w_o)


def _ffn_body(x_ref, sh_ref, sc_ref, gt_ref, g_ref, gf_ref, w1_ref, w2_ref, o_ref, u_ref):
    f = pl.program_id(2)

    @pl.when(f == 0)
    def _():
        u_ref[...] = _modulated_norm(x_ref[0], g_ref[...], sc_ref[0], sh_ref[0]).astype(BF16)

    h = jnp.dot(u_ref[...], w1_ref[...], preferred_element_type=F32)
    h = jnp.square(jnp.maximum(h, 0.0)).astype(BF16)
    part = jnp.dot(h, w2_ref[...], preferred_element_type=F32)

    @pl.when(f == 0)
    def _():
        o_ref[0] = part

    @pl.when(f > 0)
    def _():
        o_ref[0] += part

    @pl.when(f == pl.num_programs(2) - 1)
    def _():
        x2 = x_ref[0] + gt_ref[0] * o_ref[0]
        ms = jnp.mean(x2 * x2, axis=-1, keepdims=True)
        o_ref[0] = x2 * lax.rsqrt(ms + NORM_EPS) * gf_ref[...]


def _ffn(x, sh, sc, gt, g_ffn, g_final, w_ff1, w_ff2):
    b, l, d = x.shape
    tm = min(FFN_TM, l)
    d_ff = w_ff1.shape[1]
    tf = FFN_TF
    return pl.pallas_call(
        _ffn_body,
        grid=(b, l // tm, d_ff // tf),
        in_specs=[pl.BlockSpec((1, tm, d), lambda bi, i, f: (bi, i, 0)),
                  pl.BlockSpec((1, 1, d), lambda bi, i, f: (bi, 0, 0)),
                  pl.BlockSpec((1, 1, d), lambda bi, i, f: (bi, 0, 0)),
                  pl.BlockSpec((1, 1, d), lambda bi, i, f: (bi, 0, 0)),
                  pl.BlockSpec((1, d), lambda bi, i, f: (0, 0)),
                  pl.BlockSpec((1, d), lambda bi, i, f: (0, 0)),
                  pl.BlockSpec((d, tf), lambda bi, i, f: (0, f)),
                  pl.BlockSpec((tf, d), lambda bi, i, f: (f, 0))],
        out_specs=pl.BlockSpec((1, tm, d), lambda bi, i, f: (bi, i, 0)),
        out_shape=jax.ShapeDtypeStruct((b, l, d), F32),
        scratch_shapes=[pltpu.VMEM((tm, d), BF16)],
        compiler_params=_params("parallel", "parallel", "arbitrary"),
        name="ffn",
    )(x, sh, sc, gt, g_ffn, g_final, w_ff1, w_ff2)


def _layer(x, mod, g_mix, w_in_r, na_tab, dil_tab, w_na_out, w_dil_out, w_o, g_ffn, w_ff1, w_ff2, g_final):
    d = x.shape[-1]
    sh1, sc1, gt1, sh2, sc2, gt2 = [mod[:, None, k * d:(k + 1) * d] for k in range(6)]
    z = _in_proj(x, sh1, sc1, g_mix, w_in_r)
    o_na = _na_attn(z, na_tab)
    o_dl = _dil_attn(z, dil_tab)
    x1 = _out_proj(o_na, o_dl, z, x, gt1, w_na_out, w_dil_out, w_o)
    return _ffn(x1, sh2, sc2, gt2, g_ffn, g_final, w_ff1, w_ff2)


def kernel(x_prompt, x_sample, c_prompt, c_sample, w_mod, b_mod, g_mix, w_in, rpb, w_na_out, w_dil_out, w_o,
           g_ffn, w_ff1, w_ff2, g_final):
    assert w_mod.shape[0] == 1, "single-layer trunk"
    d = x_prompt.shape[-1]
    nb_p, nb_s = c_prompt.shape[0], c_sample.shape[0]
    pad = (-(nb_p + nb_s)) % 8
    c_all = jnp.concatenate([c_prompt, c_sample, jnp.zeros((pad, d), F32)], axis=0)
    mod = _modulation(c_all, w_mod[0], b_mod[0])

    w_in_r = jnp.concatenate([w_in[0][:, c0:c0 + n] for c0, n in _w_in_column_order(d)], axis=1).astype(BF16)
    shared = (g_mix[0][None], w_in_r, _na_bias_tables(rpb[0]), _dil_bias_tables(),
              w_na_out[0].astype(BF16), w_dil_out[0].astype(BF16), w_o[0].astype(BF16),
              g_ffn[0][None], w_ff1[0].astype(BF16), w_ff2[0].astype(BF16), g_final[None])
    y_prompt = _layer(x_prompt, mod[:nb_p], *shared)
    y_sample = _layer(x_sample, mod[nb_p:nb_p + nb_s], *shared)
    return (y_prompt, y_sample)
```

```python
import functools

import numpy as np
import jax
import jax.numpy as jnp
from jax import lax
from jax.experimental import pallas as pl
from jax.experimental.pallas import tpu as pltpu

F32 = jnp.float32
BF16 = jnp.bfloat16

HEAD_DIM = 128
N_HEADS_NA = 4
NA_ROWS = 8
NA_KW = 16
GRID_W = 64
DIL_GROUPS = ((128, 1), (512, 4), (2048, 16))
N_HEADS_PER_DIL = 4
N_HEADS_DIL = N_HEADS_PER_DIL * len(DIL_GROUPS)
ALIBI_MAX_BIAS = 8.0
NORM_EPS = 1e-6
NEG_INF = -1e30
ATTN_SCALE = HEAD_DIM ** -0.5

VMEM_LIMIT_BYTES = 60 * 1024 * 1024

ZH_G_NA, ZH_G_DL = 0, 16
ZH_Q_NA, ZH_K_NA, ZH_V_NA = 32, 36, 40
ZH_Q_DL = (44, 56, 68)
ZH_K_DL = (48, 60, 72)
ZH_V_DL = (52, 64, 76)
Z_HEADS = 80

IN_TM = 1024
IN_TN = 2048
IN_TILE_KINDS = ((0, 0, 0, 0), (0, 0, 0, 0), (0, 0, 0, 0), (0, 0, 1, 1), (1, 2, 2, 2))

NA_TQ = 1024
NA_GROUP = 4 * GRID_W
NA_KEYS = 12 * GRID_W
DIL_TQ = 2048
DIL_Q = 128
DIL_K = 256
DIL_UNROLL = 16
NA_UNROLL = 4
OUT_TM = 512
NORM_ROWS = 64
FFN_TM = 1024
FFN_TF = 512
FFN_TN = 512


def _params(*sem):
    return pltpu.CompilerParams(dimension_semantics=sem, vmem_limit_bytes=VMEM_LIMIT_BYTES)


def _w_in_column_order(d_model):
    d_na = N_HEADS_NA * HEAD_DIM
    d_dil = N_HEADS_DIL * HEAD_DIM
    d_grp = N_HEADS_PER_DIL * HEAD_DIM
    q_na, k_na, v_na = 0, d_na, 2 * d_na
    q_dl, k_dl, v_dl = 3 * d_na, 3 * d_na + d_dil, 3 * d_na + 2 * d_dil
    g_na = 3 * d_na + 3 * d_dil
    g_dl = g_na + d_model
    order = [(g_na, d_model), (g_dl, d_model), (q_na, d_na), (k_na, d_na), (v_na, d_na)]
    for g in range(len(DIL_GROUPS)):
        order += [(q_dl + g * d_grp, d_grp), (k_dl + g * d_grp, d_grp), (v_dl + g * d_grp, d_grp)]
    return order


def _mod_body(c_ref, w_ref, b_ref, o_ref):
    c = c_ref[...]
    s = (c * jax.nn.sigmoid(c)).astype(BF16)
    o_ref[...] = jnp.dot(s, w_ref[...].astype(BF16), preferred_element_type=F32) + b_ref[...]


def _modulation(c, w_mod, b_mod):
    rows, d = c.shape
    n = w_mod.shape[1]
    tn = 1024
    return pl.pallas_call(
        _mod_body,
        grid=(n // tn,),
        in_specs=[pl.BlockSpec((rows, d), lambda j: (0, 0)),
                  pl.BlockSpec((d, tn), lambda j: (0, j)),
                  pl.BlockSpec((1, tn), lambda j: (0, j))],
        out_specs=pl.BlockSpec((rows, tn), lambda j: (0, j)),
        out_shape=jax.ShapeDtypeStruct((rows, n), F32),
        compiler_params=_params("parallel"),
        name="modulation",
    )(c, w_mod, b_mod.reshape(1, n))


def _modulated_norm_rows(x_ref, u_ref, g, sc, sh, on_chunk=None):
    tm = u_ref.shape[0]
    gs = g * (1.0 + sc)

    def chunk(r, carry):
        r0 = pl.multiple_of(r * NORM_ROWS, NORM_ROWS)
        x = x_ref[0, pl.ds(r0, NORM_ROWS), :]
        rinv = lax.rsqrt(jnp.mean(x * x, axis=-1, keepdims=True) + NORM_EPS)
        u_ref[pl.ds(r0, NORM_ROWS), :] = (x_ref[0, pl.ds(r0, NORM_ROWS), :] * rinv * gs + sh).astype(BF16)
        if on_chunk is not None:
            on_chunk(r0)
        return carry
    lax.fori_loop(0, tm // NORM_ROWS, chunk, 0, unroll=2)


def _masked_softmax_pv(s, v):
    m = jnp.max(s, axis=-1, keepdims=True)
    p = jnp.exp(s - m)
    l = jnp.sum(p, axis=-1, keepdims=True)
    o = jnp.dot(p.astype(BF16), v, preferred_element_type=F32) / l
    return o, m + jnp.log(l)


def _qk(q, k):
    return lax.dot_general(q, k, (((1,), (1,)), ((), ())), preferred_element_type=F32)


def _in_proj_body(x_ref, sh_ref, sc_ref, g_ref, w_ref, z_ref, u_ref, p_ref):
    j = pl.program_id(2)
    tm = u_ref.shape[0]

    @pl.when(j == 0)
    def _():
        _modulated_norm_rows(x_ref, u_ref, g_ref[...], sc_ref[0], sh_ref[0])

    def subtile(s, kind):
        acc = jnp.dot(u_ref[...], w_ref[:, s * 512:(s + 1) * 512], preferred_element_type=F32)
        for hh in range(4):
            a = acc[:, hh * HEAD_DIM:(hh + 1) * HEAD_DIM]
            zh = s * 4 + hh
            if kind == 0:
                z_ref[0, zh] = a.astype(BF16)
                continue
            p_ref[hh] = a
            if kind == 1:
                for sb in range(tm // 256):
                    for e in range(4):
                        r0 = sb * 256 + e * 64
                        z_ref[0, zh, r0:r0 + 64, :] = p_ref[hh, pl.ds(sb * 256 + e, 64, stride=4), :].astype(BF16)
            else:
                for sb in range(tm // 1024):
                    for e in range(16):
                        r0 = sb * 1024 + e * 64
                        z_ref[0, zh, r0:r0 + 64, :] = p_ref[hh, pl.ds(sb * 1024 + e, 64, stride=16), :].astype(BF16)

    @pl.when(j < 3)
    def _():
        for s in range(4):
            subtile(s, 0)

    for jj in (3, 4):
        @pl.when(j == jj)
        def _(jj=jj):
            for s in range(4):
                subtile(s, IN_TILE_KINDS[jj][s])


def _in_proj(x, sh, sc, g, w_in_r):
    b, l, d = x.shape
    tm = min(IN_TM, l)
    n_tiles = w_in_r.shape[1] // IN_TN
    assert n_tiles == len(IN_TILE_KINDS) and l % tm == 0 and tm % 1024 == 0
    return pl.pallas_call(
        _in_proj_body,
        grid=(b, l // tm, n_tiles),
        in_specs=[pl.BlockSpec((1, tm, d), lambda bi, i, j: (bi, i, 0)),
                  pl.BlockSpec((1, 1, d), lambda bi, i, j: (bi, 0, 0)),
                  pl.BlockSpec((1, 1, d), lambda bi, i, j: (bi, 0, 0)),
                  pl.BlockSpec((1, d), lambda bi, i, j: (0, 0)),
                  pl.BlockSpec((d, IN_TN), lambda bi, i, j: (0, j))],
        out_specs=pl.BlockSpec((1, IN_TN // HEAD_DIM, tm, HEAD_DIM), lambda bi, i, j: (bi, j, i, 0)),
        out_shape=jax.ShapeDtypeStruct((b, Z_HEADS, l, HEAD_DIM), BF16),
        scratch_shapes=[pltpu.VMEM((tm, d), BF16), pltpu.VMEM((4, tm, HEAD_DIM), F32)],
        compiler_params=_params("parallel", "parallel", "arbitrary"),
        name="in_proj",
    )(x, sh, sc, g, w_in_r)


def _na_bias_tables(rpb):
    rho = np.arange(4)[:, None, None, None]
    qc = np.arange(GRID_W)[None, :, None, None]
    f = np.arange(12)[None, None, :, None]
    kc = np.arange(GRID_W)[None, None, None, :]
    col_start = np.clip(qc - NA_KW // 2, 0, GRID_W - NA_KW)
    col_ok = (kc >= col_start) & (kc < col_start + NA_KW)
    row_ok = np.stack([
        (f >= rho) & (f < rho + NA_ROWS),
        (f >= 4) & (f < 4 + NA_ROWS) & (rho >= 0),
        (f >= 0) & (f < NA_ROWS) & (rho >= 0),
    ])
    ok = row_ok & col_ok[None]
    padded = jnp.pad(rpb.astype(F32), ((0, 0), (0, 0), (GRID_W, GRID_W)))
    c0 = GRID_W + NA_KW - 1
    col = jnp.stack([padded[:, :, c0 - q:c0 - q + GRID_W] for q in range(GRID_W)], axis=2)
    bias = jnp.stack([jnp.stack([col[:, fr - r + 3] for fr in range(12)], axis=2) for r in range(4)], axis=1)
    tab = jnp.where(ok[:, None], bias[None], NEG_INF)
    return tab.reshape(3, rpb.shape[0], NA_GROUP, NA_KEYS)


def _na_body(q_ref, k_ref, kp_ref, kn_ref, v_ref, vp_ref, vn_ref, tab_ref, o_ref, kbuf, vbuf, *, n_groups):
    i = pl.program_id(2)
    tq = q_ref.shape[2]
    halo = kp_ref.shape[2]
    kbuf[0:halo, :] = kp_ref[0, 0]
    kbuf[halo:halo + tq, :] = k_ref[0, 0]
    kbuf[halo + tq:, :] = kn_ref[0, 0]
    vbuf[0:halo, :] = vp_ref[0, 0]
    vbuf[halo:halo + tq, :] = v_ref[0, 0]
    vbuf[halo + tq:, :] = vn_ref[0, 0]

    def group(a, carry):
        q0 = pl.multiple_of(a * NA_GROUP, NA_GROUP)
        ga = i * (tq // NA_GROUP) + a
        variant = jnp.where(ga == 0, 1, jnp.where(ga == n_groups - 1, 2, 0))
        q = q_ref[0, 0, pl.ds(q0, NA_GROUP), :]
        k = kbuf[pl.ds(q0, NA_KEYS), :]
        v = vbuf[pl.ds(q0, NA_KEYS), :]
        s = _qk(q, k) * ATTN_SCALE + tab_ref[variant, 0]
        o, _ = _masked_softmax_pv(s, v)
        o_ref[0, 0, pl.ds(q0, NA_GROUP), :] = o.astype(o_ref.dtype)
        return carry

    lax.fori_loop(0, tq // NA_GROUP, group, 0, unroll=NA_UNROLL)


def _na_attn(z, tab):
    b, _, l, _ = z.shape
    tq = min(NA_TQ, l)
    halo = NA_GROUP
    n_halo_blocks = l // halo
    r = tq // halo
    assert l % tq == 0 and l // NA_GROUP >= 3

    def cur(h0):
        return pl.BlockSpec((1, 1, tq, HEAD_DIM), lambda bi, h, i: (bi, h0 + h, i, 0))

    def prev(h0):
        return pl.BlockSpec((1, 1, halo, HEAD_DIM), lambda bi, h, i: (bi, h0 + h, jnp.maximum(i * r - 1, 0), 0))

    def nxt(h0):
        return pl.BlockSpec((1, 1, halo, HEAD_DIM),
                            lambda bi, h, i: (bi, h0 + h, jnp.minimum((i + 1) * r, n_halo_blocks - 1), 0))

    return pl.pallas_call(
        functools.partial(_na_body, n_groups=l // NA_GROUP),
        grid=(b, N_HEADS_NA, l // tq),
        in_specs=[cur(ZH_Q_NA), cur(ZH_K_NA), prev(ZH_K_NA), nxt(ZH_K_NA),
                  cur(ZH_V_NA), prev(ZH_V_NA), nxt(ZH_V_NA),
                  pl.BlockSpec((3, 1, NA_GROUP, NA_KEYS), lambda bi, h, i: (0, h, 0, 0))],
        out_specs=pl.BlockSpec((1, 1, tq, HEAD_DIM), lambda bi, h, i: (bi, h, i, 0)),
        out_shape=jax.ShapeDtypeStruct((b, N_HEADS_NA, l, HEAD_DIM), BF16),
        scratch_shapes=[pltpu.VMEM((tq + 2 * halo, HEAD_DIM), BF16)] * 2,
        compiler_params=_params("parallel", "parallel", "parallel"),
        name="na_attn",
    )(z, z, z, z, z, z, z, tab)


def _dil_bias_tables():
    h = np.arange(1, N_HEADS_DIL + 1, dtype=np.float32)
    slopes = np.power(np.float32(2.0), -ALIBI_MAX_BIAS * h / N_HEADS_DIL).astype(np.float32)
    rel = np.arange(DIL_K)[None, :] - (DIL_K - DIL_Q) // 2 - np.arange(DIL_Q)[:, None]
    tabs = []
    for g, (window, dilation) in enumerate(DIL_GROUPS):
        n = window // (2 * dilation)
        assert n == (DIL_K - DIL_Q) // 2
        dist = (np.abs(rel) * dilation).astype(np.float32)
        for jh in range(N_HEADS_PER_DIL):
            bias = -slopes[g * N_HEADS_PER_DIL + jh] * dist
            tabs.append(np.where(np.abs(rel) <= n, bias, np.float32(NEG_INF)))
    return jnp.asarray(np.stack(tabs).astype(np.float32))


def _dil_body(q0_ref, k0_ref, k0p_ref, k0n_ref, v0_ref, v0p_ref, v0n_ref,
              q1_ref, k1_ref, k1p_ref, k1n_ref, v1_ref, v1p_ref, v1n_ref,
              q2_ref, k2_ref, k2p_ref, k2n_ref, v2_ref, v2p_ref, v2n_ref,
              b0_ref, b1_ref, b2_ref, o_ref, qbuf, kbuf, vbuf, o_scr, l_scr, *, seq_len):
    i = pl.program_id(1)
    tq = q0_ref.shape[2]
    halo = (DIL_K - DIL_Q) // 2

    def unit(q, k, v, bias, key_lo, n_sub):
        kpos = key_lo + lax.broadcasted_iota(jnp.int32, (1, DIL_K), 1)
        s = _qk(q, k) * ATTN_SCALE + bias
        s = jnp.where((kpos >= 0) & (kpos < n_sub), s, NEG_INF)
        o, lse = _masked_softmax_pv(s, v)
        return o, jnp.broadcast_to(lse, (DIL_Q, HEAD_DIM))

    kbuf[0:halo, :] = k0p_ref[0, 0]
    kbuf[halo:halo + tq, :] = k0_ref[0, 0]
    kbuf[halo + tq:2 * halo + tq, :] = k0n_ref[0, 0]
    vbuf[0:halo, :] = v0p_ref[0, 0]
    vbuf[halo:halo + tq, :] = v0_ref[0, 0]
    vbuf[halo + tq:2 * halo + tq, :] = v0n_ref[0, 0]

    def unit0(c, carry):
        r0 = pl.multiple_of(c * DIL_Q, DIL_Q)
        o, lse = unit(q0_ref[0, 0, pl.ds(r0, DIL_Q), :], kbuf[pl.ds(r0, DIL_K), :], vbuf[pl.ds(r0, DIL_K), :],
                      b0_ref[0], i * tq + r0 - halo, seq_len)
        o_scr[0, pl.ds(r0, DIL_Q), :] = o
        l_scr[0, pl.ds(r0, DIL_Q), :] = lse
        return carry
    lax.fori_loop(0, tq // DIL_Q, unit0, 0, unroll=DIL_UNROLL)

    d = 4
    n_sb = tq // 256
    per_res_q = tq // d
    per_res_k = per_res_q + 2 * halo
    for e in range(d):
        for sb in range(n_sb):
            src = sb * 256 + e * 64
            qbuf[e * per_res_q + sb * 64:e * per_res_q + (sb + 1) * 64, :] = q1_ref[0, 0, src:src + 64, :]
            kbuf[e * per_res_k + halo + sb * 64:e * per_res_k + halo + (sb + 1) * 64, :] = k1_ref[0, 0, src:src + 64, :]
            vbuf[e * per_res_k + halo + sb * 64:e * per_res_k + halo + (sb + 1) * 64, :] = v1_ref[0, 0, src:src + 64, :]
        kbuf[e * per_res_k:e * per_res_k + halo, :] = k1p_ref[0, 0, e * 64:(e + 1) * 64, :]
        vbuf[e * per_res_k:e * per_res_k + halo, :] = v1p_ref[0, 0, e * 64:(e + 1) * 64, :]
        kbuf[e * per_res_k + halo + per_res_q:(e + 1) * per_res_k, :] = k1n_ref[0, 0, e * 64:(e + 1) * 64, :]
        vbuf[e * per_res_k + halo + per_res_q:(e + 1) * per_res_k, :] = v1n_ref[0, 0, e * 64:(e + 1) * 64, :]

    n_c1 = per_res_q // DIL_Q

    def unit1(t, carry):
        e = t // n_c1
        c = t % n_c1
        qo = pl.multiple_of(e * per_res_q + c * DIL_Q, DIL_Q)
        ko = pl.multiple_of(e * per_res_k + c * DIL_Q, DIL_Q)
        o, lse = unit(qbuf[pl.ds(qo, DIL_Q), :], kbuf[pl.ds(ko, DIL_K), :], vbuf[pl.ds(ko, DIL_K), :],
                      b1_ref[0], i * per_res_q + c * DIL_Q - halo, seq_len // d)
        t0 = c * (DIL_Q * d) + e
        o_scr[1, pl.ds(t0, DIL_Q, stride=d), :] = o
        l_scr[1, pl.ds(t0, DIL_Q, stride=d), :] = lse
        return carry
    lax.fori_loop(0, d * n_c1, unit1, 0, unroll=DIL_UNROLL)

    d = 16
    n_blk = tq // 1024
    per_res_q = tq // d
    per_res_k = per_res_q + 2 * halo
    for e in range(d):
        for sb in range(n_blk):
            src = sb * 1024 + e * 64
            qbuf[e * per_res_q + sb * 64:e * per_res_q + (sb + 1) * 64, :] = q2_ref[0, 0, src:src + 64, :]
            kbuf[e * per_res_k + halo + sb * 64:e * per_res_k + halo + (sb + 1) * 64, :] = k2_ref[0, 0, src:src + 64, :]
            vbuf[e * per_res_k + halo + sb * 64:e * per_res_k + halo + (sb + 1) * 64, :] = v2_ref[0, 0, src:src + 64, :]
        kbuf[e * per_res_k:e * per_res_k + halo, :] = k2p_ref[0, 0, e * 64:(e + 1) * 64, :]
        vbuf[e * per_res_k:e * per_res_k + halo, :] = v2p_ref[0, 0, e * 64:(e + 1) * 64, :]
        kbuf[e * per_res_k + halo + per_res_q:(e + 1) * per_res_k, :] = k2n_ref[0, 0, e * 64:(e + 1) * 64, :]
        vbuf[e * per_res_k + halo + per_res_q:(e + 1) * per_res_k, :] = v2n_ref[0, 0, e * 64:(e + 1) * 64, :]

    n_c2 = per_res_q // DIL_Q

    def unit2(t, carry):
        e = t // n_c2
        c = t % n_c2
        qo = pl.multiple_of(e * per_res_q + c * DIL_Q, DIL_Q)
        ko = pl.multiple_of(e * per_res_k + c * DIL_Q, DIL_Q)
        o, lse = unit(qbuf[pl.ds(qo, DIL_Q), :], kbuf[pl.ds(ko, DIL_K), :], vbuf[pl.ds(ko, DIL_K), :],
                      b2_ref[0], i * per_res_q + c * DIL_Q - halo, seq_len // d)
        t0 = c * (DIL_Q * d) + e
        o_scr[2, pl.ds(t0, DIL_Q, stride=d), :] = o
        l_scr[2, pl.ds(t0, DIL_Q, stride=d), :] = lse
        return carry
    lax.fori_loop(0, d * n_c2, unit2, 0, unroll=DIL_UNROLL)

    rows = 256

    def merge(r, carry):
        r0 = pl.multiple_of(r * rows, rows)
        l0 = l_scr[0, pl.ds(r0, rows), :]
        l1 = l_scr[1, pl.ds(r0, rows), :]
        l2 = l_scr[2, pl.ds(r0, rows), :]
        m = jnp.maximum(jnp.maximum(l0, l1), l2)
        w0, w1, w2 = jnp.exp(l0 - m), jnp.exp(l1 - m), jnp.exp(l2 - m)
        den = w0 + w1 + w2
        num = (w0 / den) * o_scr[0, pl.ds(r0, rows), :] + (w1 / den) * o_scr[1, pl.ds(r0, rows), :] \
            + (w2 / den) * o_scr[2, pl.ds(r0, rows), :]
        o_ref[0, 0, pl.ds(r0, rows), :] = num.astype(o_ref.dtype)
        return carry
    lax.fori_loop(0, tq // rows, merge, 0)


def _dil_attn(z, tab):
    b, _, l, _ = z.shape
    tq = min(DIL_TQ, l)
    halo = (DIL_K - DIL_Q) // 2
    assert l % tq == 0 and tq % 2048 == 0

    def cur(h0):
        return pl.BlockSpec((1, 1, tq, HEAD_DIM), lambda bi, i, h: (bi, h0 + h, i, 0))

    def prev(h0, rows):
        r = tq // rows
        return pl.BlockSpec((1, 1, rows, HEAD_DIM), lambda bi, i, h: (bi, h0 + h, jnp.maximum(i * r - 1, 0), 0))

    def nxt(h0, rows):
        r = tq // rows
        nblk = l // rows
        return pl.BlockSpec((1, 1, rows, HEAD_DIM),
                            lambda bi, i, h: (bi, h0 + h, jnp.minimum((i + 1) * r, nblk - 1), 0))

    in_specs = []
    for g, (_, dilation) in enumerate(DIL_GROUPS):
        hrows = halo * dilation
        in_specs += [cur(ZH_Q_DL[g]),
                     cur(ZH_K_DL[g]), prev(ZH_K_DL[g], hrows), nxt(ZH_K_DL[g], hrows),
                     cur(ZH_V_DL[g]), prev(ZH_V_DL[g], hrows), nxt(ZH_V_DL[g], hrows)]
    for g in range(len(DIL_GROUPS)):
        in_specs.append(pl.BlockSpec((1, DIL_Q, DIL_K), lambda bi, i, h, g=g: (g * N_HEADS_PER_DIL + h, 0, 0)))

    buf_rows = 16 * (tq // 16 + 2 * halo)
    return pl.pallas_call(
        functools.partial(_dil_body, seq_len=l),
        grid=(b, l // tq, N_HEADS_PER_DIL),
        in_specs=in_specs,
        out_specs=pl.BlockSpec((1, 1, tq, HEAD_DIM), lambda bi, i, h: (bi, h, i, 0)),
        out_shape=jax.ShapeDtypeStruct((b, N_HEADS_PER_DIL, l, HEAD_DIM), BF16),
        scratch_shapes=[pltpu.VMEM((tq, HEAD_DIM), BF16),
                        pltpu.VMEM((buf_rows, HEAD_DIM), BF16),
                        pltpu.VMEM((buf_rows, HEAD_DIM), BF16),
                        pltpu.VMEM((3, tq, HEAD_DIM), F32),
                        pltpu.VMEM((3, tq, HEAD_DIM), F32)],
        compiler_params=_params("parallel", "parallel", "parallel"),
        name="dil_attn",
    )(*([z] * 21), tab, tab, tab)


def _out_proj_body(ona_ref, odl_ref, gna_ref, gdl_ref, x_ref, gt_ref, wna_ref, wdl_ref, wo_ref, o_ref):
    def heads(ref):
        return jnp.concatenate([ref[0, h] for h in range(ref.shape[1])], axis=-1)

    y_na = jnp.dot(heads(ona_ref), wna_ref[...], preferred_element_type=F32)
    y_dl = jnp.dot(heads(odl_ref), wdl_ref[...], preferred_element_type=F32)
    merged = jax.nn.sigmoid(heads(gna_ref).astype(F32)) * y_na + jax.nn.sigmoid(heads(gdl_ref).astype(F32)) * y_dl
    mix = jnp.dot(merged.astype(BF16), wo_ref[...], preferred_element_type=F32)
    o_ref[0] = x_ref[0] + gt_ref[0] * mix


def _out_proj(o_na, o_dl, z, x, gt, w_na_out, w_dil_out, w_o):
    b, l, d = x.shape
    tm = min(OUT_TM, l)
    n_gate_heads = d // HEAD_DIM
    assert ZH_G_NA == 0 and ZH_G_DL == n_gate_heads

    def const(shape):
        return pl.BlockSpec(shape, lambda bi, i: (0, 0), pipeline_mode=pl.Buffered(1))

    return pl.pallas_call(
        _out_proj_body,
        grid=(b, l // tm),
        in_specs=[pl.BlockSpec((1, N_HEADS_NA, tm, HEAD_DIM), lambda bi, i: (bi, 0, i, 0)),
                  pl.BlockSpec((1, N_HEADS_PER_DIL, tm, HEAD_DIM), lambda bi, i: (bi, 0, i, 0)),
                  pl.BlockSpec((1, n_gate_heads, tm, HEAD_DIM), lambda bi, i: (bi, 0, i, 0)),
                  pl.BlockSpec((1, n_gate_heads, tm, HEAD_DIM), lambda bi, i: (bi, 1, i, 0)),
                  pl.BlockSpec((1, tm, d), lambda bi, i: (bi, i, 0)),
                  pl.BlockSpec((1, 1, d), lambda bi, i: (bi, 0, 0)),
                  const(w_na_out.shape), const(w_dil_out.shape), const(w_o.shape)],
        out_specs=pl.BlockSpec((1, tm, d), lambda bi, i: (bi, i, 0)),
        out_shape=jax.ShapeDtypeStruct((b, l, d), F32),
        compiler_params=_params("parallel", "parallel"),
        name="out_proj",
    )(o_na, o_dl, z, z, x, gt, w_na_out, w_dil_out, w_o)


def _ffn_body(x_ref, sh_ref, sc_ref, gt_ref, g_ref, gf_ref, w1_ref, w2_ref, o_ref, u_ref):
    f = pl.program_id(2)

    tm, d = u_ref.shape
    rows = NORM_ROWS

    @pl.when(f == 0)
    def _():
        def zero_acc(r0):
            o_ref[0, pl.ds(r0, rows), :] = jnp.zeros((rows, d), F32)
        _modulated_norm_rows(x_ref, u_ref, g_ref[...], sc_ref[0], sh_ref[0], on_chunk=zero_acc)

    h = jnp.dot(u_ref[...], w1_ref[...], preferred_element_type=F32)
    h = jnp.square(jnp.maximum(h, 0.0)).astype(BF16)
    for n0 in range(0, d, FFN_TN):
        o_ref[0, :, n0:n0 + FFN_TN] += jnp.dot(h, w2_ref[:, n0:n0 + FFN_TN], preferred_element_type=F32)

    @pl.when(f == pl.num_programs(2) - 1)
    def _():
        def chunk(r, carry):
            r0 = pl.multiple_of(r * rows, rows)
            x2 = x_ref[0, pl.ds(r0, rows), :] + gt_ref[0] * o_ref[0, pl.ds(r0, rows), :]
            o_ref[0, pl.ds(r0, rows), :] = x2
            rinv = lax.rsqrt(jnp.mean(x2 * x2, axis=-1, keepdims=True) + NORM_EPS)
            o_ref[0, pl.ds(r0, rows), :] = o_ref[0, pl.ds(r0, rows), :] * rinv * gf_ref[...]
            return carry
        lax.fori_loop(0, tm // rows, chunk, 0, unroll=2)


def _ffn(x, sh, sc, gt, g_ffn, g_final, w_ff1, w_ff2):
    b, l, d = x.shape
    tm = min(FFN_TM, l)
    d_ff = w_ff1.shape[1]
    tf = FFN_TF
    return pl.pallas_call(
        _ffn_body,
        grid=(b, l // tm, d_ff // tf),
        in_specs=[pl.BlockSpec((1, tm, d), lambda bi, i, f: (bi, i, 0)),
                  pl.BlockSpec((1, 1, d), lambda bi, i, f: (bi, 0, 0)),
                  pl.BlockSpec((1, 1, d), lambda bi, i, f: (bi, 0, 0)),
                  pl.BlockSpec((1, 1, d), lambda bi, i, f: (bi, 0, 0)),
                  pl.BlockSpec((1, d), lambda bi, i, f: (0, 0)),
                  pl.BlockSpec((1, d), lambda bi, i, f: (0, 0)),
                  pl.BlockSpec((d, tf), lambda bi, i, f: (0, f)),
                  pl.BlockSpec((tf, d), lambda bi, i, f: (f, 0))],
        out_specs=pl.BlockSpec((1, tm, d), lambda bi, i, f: (bi, i, 0)),
        out_shape=jax.ShapeDtypeStruct((b, l, d), F32),
        scratch_shapes=[pltpu.VMEM((tm, d), BF16)],
        compiler_params=_params("parallel", "parallel", "arbitrary"),
        name="ffn",
    )(x, sh, sc, gt, g_ffn, g_final, w_ff1, w_ff2)


def _layer(x, mod, g_mix, w_in_r, na_tab, dil_tab, w_na_out, w_dil_out, w_o, g_ffn, w_ff1, w_ff2, g_final):
    d = x.shape[-1]
    sh1, sc1, gt1, sh2, sc2, gt2 = [mod[:, None, k * d:(k + 1) * d] for k in range(6)]
    z = _in_proj(x, sh1, sc1, g_mix, w_in_r)
    o_na = _na_attn(z, na_tab)
    o_dl = _dil_attn(z, dil_tab)
    x1 = _out_proj(o_na, o_dl, z, x, gt1, w_na_out, w_dil_out, w_o)
    return _ffn(x1, sh2, sc2, gt2, g_ffn, g_final, w_ff1, w_ff2)


def kernel(x_prompt, x_sample, c_prompt, c_sample, w_mod, b_mod, g_mix, w_in, rpb, w_na_out, w_dil_out, w_o,
           g_ffn, w_ff1, w_ff2, g_final):
    assert w_mod.shape[0] == 1, "single-layer trunk"
    d = x_prompt.shape[-1]
    nb_p, nb_s = c_prompt.shape[0], c_sample.shape[0]
    pad = (-(nb_p + nb_s)) % 8
    c_all = jnp.concatenate([c_prompt, c_sample, jnp.zeros((pad, d), F32)], axis=0)
    mod = _modulation(c_all, w_mod[0], b_mod[0])

    w_in_r = jnp.concatenate([w_in[0][:, c0:c0 + n] for c0, n in _w_in_column_order(d)], axis=1).astype(BF16)
    shared = (g_mix[0][None], w_in_r, _na_bias_tables(rpb[0]), _dil_bias_tables(),
              w_na_out[0].astype(BF16), w_dil_out[0].astype(BF16), w_o[0].astype(BF16),
              g_ffn[0][None], w_ff1[0].astype(BF16), w_ff2[0].astype(BF16), g_final[None])
    y_prompt = _layer(x_prompt, mod[:nb_p], *shared)
    y_sample = _layer(x_sample, mod[nb_p:nb_p + nb_s], *shared)
    return (y_prompt, y_sample)
```

```python
import functools

import numpy as np
import jax
import jax.numpy as jnp
from jax import lax
from jax.experimental import pallas as pl
from jax.experimental.pallas import tpu as pltpu

F32 = jnp.float32
BF16 = jnp.bfloat16

HEAD_DIM = 128
N_HEADS_NA = 4
NA_ROWS = 8
NA_KW = 16
GRID_W = 64
DIL_GROUPS = ((128, 1), (512, 4), (2048, 16))
N_HEADS_PER_DIL = 4
N_HEADS_DIL = N_HEADS_PER_DIL * len(DIL_GROUPS)
ALIBI_MAX_BIAS = 8.0
NORM_EPS = 1e-6
NEG_INF = -1e30
ATTN_SCALE = HEAD_DIM ** -0.5

VMEM_LIMIT_BYTES = 60 * 1024 * 1024

ZH_G_NA, ZH_G_DL = 0, 16
ZH_Q_NA, ZH_K_NA, ZH_V_NA = 32, 36, 40
ZH_Q_DL = (44, 56, 68)
ZH_K_DL = (48, 60, 72)
ZH_V_DL = (52, 64, 76)
Z_HEADS = 80

IN_TM = 1024
IN_TN = 2048
IN_SUB = 512

NA_TQ = 1024
NA_GROUP = 4 * GRID_W
NA_KEYS = 12 * GRID_W
DIL_TQ = 2048
DIL_Q = 128
DIL_K = 256
OUT_TM = 512
NORM_ROWS = 64
FFN_TM = 1024
FFN_TF = 512
FFN_TN = 512


def _params(*sem):
    return pltpu.CompilerParams(dimension_semantics=sem, vmem_limit_bytes=VMEM_LIMIT_BYTES)


def _w_in_column_order(d_model):
    d_na = N_HEADS_NA * HEAD_DIM
    d_dil = N_HEADS_DIL * HEAD_DIM
    d_grp = N_HEADS_PER_DIL * HEAD_DIM
    q_na, k_na, v_na = 0, d_na, 2 * d_na
    q_dl, k_dl, v_dl = 3 * d_na, 3 * d_na + d_dil, 3 * d_na + 2 * d_dil
    g_na = 3 * d_na + 3 * d_dil
    g_dl = g_na + d_model
    by_head = {ZH_G_NA: (g_na, d_model), ZH_G_DL: (g_dl, d_model),
               ZH_Q_NA: (q_na, d_na), ZH_K_NA: (k_na, d_na), ZH_V_NA: (v_na, d_na)}
    for g in range(len(DIL_GROUPS)):
        by_head[ZH_Q_DL[g]] = (q_dl + g * d_grp, d_grp)
        by_head[ZH_K_DL[g]] = (k_dl + g * d_grp, d_grp)
        by_head[ZH_V_DL[g]] = (v_dl + g * d_grp, d_grp)
    order, head = [], 0
    for h0 in sorted(by_head):
        assert h0 == head, "head-major layout must be contiguous"
        order.append(by_head[h0])
        head += by_head[h0][1] // HEAD_DIM
    assert head == Z_HEADS
    return order


def _mod_body(c_ref, w_ref, b_ref, o_ref):
    c = c_ref[...]
    s = (c * jax.nn.sigmoid(c)).astype(BF16)
    o_ref[...] = jnp.dot(s, w_ref[...].astype(BF16), preferred_element_type=F32) + b_ref[...]


def _modulation(c, w_mod, b_mod):
    rows, d = c.shape
    n = w_mod.shape[1]
    tn = 1024
    return pl.pallas_call(
        _mod_body,
        grid=(n // tn,),
        in_specs=[pl.BlockSpec((rows, d), lambda j: (0, 0)),
                  pl.BlockSpec((d, tn), lambda j: (0, j)),
                  pl.BlockSpec((1, tn), lambda j: (0, j))],
        out_specs=pl.BlockSpec((rows, tn), lambda j: (0, j)),
        out_shape=jax.ShapeDtypeStruct((rows, n), F32),
        compiler_params=_params("parallel"),
        name="modulation",
    )(c, w_mod, b_mod.reshape(1, n))


def _modulated_norm_rows(x_ref, u_ref, g, sc, sh, on_chunk=None):
    tm = u_ref.shape[0]
    gs = g * (1.0 + sc)

    def chunk(r, carry):
        r0 = pl.multiple_of(r * NORM_ROWS, NORM_ROWS)
        x = x_ref[0, pl.ds(r0, NORM_ROWS), :]
        rinv = lax.rsqrt(jnp.mean(x * x, axis=-1, keepdims=True) + NORM_EPS)
        u_ref[pl.ds(r0, NORM_ROWS), :] = (x_ref[0, pl.ds(r0, NORM_ROWS), :] * rinv * gs + sh).astype(BF16)
        if on_chunk is not None:
            on_chunk(r0)
        return carry
    lax.fori_loop(0, tm // NORM_ROWS, chunk, 0, unroll=2)


def _masked_softmax_pv(s, v):
    m = jnp.max(s, axis=-1, keepdims=True)
    p = jnp.exp(s - m)
    l = jnp.sum(p, axis=-1, keepdims=True)
    o = jnp.dot(p.astype(BF16), v, preferred_element_type=F32) / l
    return o, m + jnp.log(l)


def _qk(q, k):
    return lax.dot_general(q, k, (((1,), (1,)), ((), ())), preferred_element_type=F32)


def _in_proj_body(x_ref, sh_ref, sc_ref, g_ref, w_ref, z_ref, u_ref):
    @pl.when(pl.program_id(2) == 0)
    def _():
        _modulated_norm_rows(x_ref, u_ref, g_ref[...], sc_ref[0], sh_ref[0])

    heads_per_dot = IN_SUB // HEAD_DIM
    for s in range(IN_TN // IN_SUB):
        acc = jnp.dot(u_ref[...], w_ref[0, :, s * IN_SUB:(s + 1) * IN_SUB], preferred_element_type=F32)
        for hh in range(heads_per_dot):
            z_ref[0, s * heads_per_dot + hh] = acc[:, hh * HEAD_DIM:(hh + 1) * HEAD_DIM].astype(BF16)


def _in_proj(x, sh, sc, g, w_in_r):
    b, l, d = x.shape
    tm = min(IN_TM, l)
    n_tiles = w_in_r.shape[0]
    assert l % tm == 0 and w_in_r.shape[1:] == (d, IN_TN)
    return pl.pallas_call(
        _in_proj_body,
        grid=(b, l // tm, n_tiles),
        in_specs=[pl.BlockSpec((1, tm, d), lambda bi, i, j: (bi, i, 0)),
                  pl.BlockSpec((1, 1, d), lambda bi, i, j: (bi, 0, 0)),
                  pl.BlockSpec((1, 1, d), lambda bi, i, j: (bi, 0, 0)),
                  pl.BlockSpec((1, d), lambda bi, i, j: (0, 0)),
                  pl.BlockSpec((1, d, IN_TN), lambda bi, i, j: (j, 0, 0))],
        out_specs=pl.BlockSpec((1, IN_TN // HEAD_DIM, tm, HEAD_DIM), lambda bi, i, j: (bi, j, i, 0)),
        out_shape=jax.ShapeDtypeStruct((b, Z_HEADS, l, HEAD_DIM), BF16),
        scratch_shapes=[pltpu.VMEM((tm, d), BF16)],
        compiler_params=_params("parallel", "parallel", "arbitrary"),
        name="in_proj",
    )(x, sh, sc, g, w_in_r)


def _na_bias_tables(rpb):
    rho = np.arange(4)[:, None, None, None]
    qc = np.arange(GRID_W)[None, :, None, None]
    f = np.arange(12)[None, None, :, None]
    kc = np.arange(GRID_W)[None, None, None, :]
    col_start = np.clip(qc - NA_KW // 2, 0, GRID_W - NA_KW)
    col_ok = (kc >= col_start) & (kc < col_start + NA_KW)
    row_ok = np.stack([
        (f >= rho) & (f < rho + NA_ROWS),
        (f >= 4) & (f < 4 + NA_ROWS) & (rho >= 0),
        (f >= 0) & (f < NA_ROWS) & (rho >= 0),
    ])
    ok = row_ok & col_ok[None]
    padded = jnp.pad(rpb.astype(F32), ((0, 0), (0, 0), (GRID_W, GRID_W)))
    c0 = GRID_W + NA_KW - 1
    col = jnp.stack([padded[:, :, c0 - q:c0 - q + GRID_W] for q in range(GRID_W)], axis=2)
    bias = jnp.stack([jnp.stack([col[:, fr - r + 3] for fr in range(12)], axis=2) for r in range(4)], axis=1)
    tab = jnp.where(ok[:, None], bias[None], NEG_INF)
    return tab.reshape(3, rpb.shape[0], NA_GROUP, NA_KEYS)


def _na_body(q_ref, k_ref, kp_ref, kn_ref, v_ref, vp_ref, vn_ref, tab_ref, o_ref):
    i = pl.program_id(2)
    tq = q_ref.shape[2]
    n_grp = tq // NA_GROUP

    def window(cur, prv, nxt, a):
        pieces = []
        for ga in (a - 1, a, a + 1):
            if ga < 0:
                pieces.append(prv[0, 0])
            elif ga >= n_grp:
                pieces.append(nxt[0, 0])
            else:
                pieces.append(cur[0, 0, ga * NA_GROUP:(ga + 1) * NA_GROUP, :])
        return jnp.concatenate(pieces, axis=0)

    for a in range(n_grp):
        variant = 0
        if a == 0:
            variant = jnp.where(i == 0, 1, variant)
        if a == n_grp - 1:
            variant = jnp.where(i == pl.num_programs(2) - 1, 2, variant)
        q = q_ref[0, 0, a * NA_GROUP:(a + 1) * NA_GROUP, :]
        s = _qk(q, window(k_ref, kp_ref, kn_ref, a)) * ATTN_SCALE + tab_ref[variant, 0]
        o, _ = _masked_softmax_pv(s, window(v_ref, vp_ref, vn_ref, a))
        o_ref[0, 0, a * NA_GROUP:(a + 1) * NA_GROUP, :] = o.astype(o_ref.dtype)


def _na_attn(z, tab):
    b, _, l, _ = z.shape
    tq = min(NA_TQ, l)
    halo = NA_GROUP
    n_halo_blocks = l // halo
    r = tq // halo
    assert l % tq == 0 and l // NA_GROUP >= 3

    def cur(h0):
        return pl.BlockSpec((1, 1, tq, HEAD_DIM), lambda bi, h, i: (bi, h0 + h, i, 0))

    def prev(h0):
        return pl.BlockSpec((1, 1, halo, HEAD_DIM), lambda bi, h, i: (bi, h0 + h, jnp.maximum(i * r - 1, 0), 0))

    def nxt(h0):
        return pl.BlockSpec((1, 1, halo, HEAD_DIM),
                            lambda bi, h, i: (bi, h0 + h, jnp.minimum((i + 1) * r, n_halo_blocks - 1), 0))

    return pl.pallas_call(
        _na_body,
        grid=(b, N_HEADS_NA, l // tq),
        in_specs=[cur(ZH_Q_NA), cur(ZH_K_NA), prev(ZH_K_NA), nxt(ZH_K_NA),
                  cur(ZH_V_NA), prev(ZH_V_NA), nxt(ZH_V_NA),
                  pl.BlockSpec((3, 1, NA_GROUP, NA_KEYS), lambda bi, h, i: (0, h, 0, 0))],
        out_specs=pl.BlockSpec((1, 1, tq, HEAD_DIM), lambda bi, h, i: (bi, h, i, 0)),
        out_shape=jax.ShapeDtypeStruct((b, N_HEADS_NA, l, HEAD_DIM), BF16),
        compiler_params=_params("parallel", "parallel", "parallel"),
        name="na_attn",
    )(z, z, z, z, z, z, z, tab)


def _dil_bias_tables():
    h = np.arange(1, N_HEADS_DIL + 1, dtype=np.float32)
    slopes = np.power(np.float32(2.0), -ALIBI_MAX_BIAS * h / N_HEADS_DIL).astype(np.float32)
    rel = np.arange(DIL_K)[None, :] - (DIL_K - DIL_Q) // 2 - np.arange(DIL_Q)[:, None]
    tabs = []
    for g, (window, dilation) in enumerate(DIL_GROUPS):
        n = window // (2 * dilation)
        assert n == (DIL_K - DIL_Q) // 2
        dist = (np.abs(rel) * dilation).astype(np.float32)
        for jh in range(N_HEADS_PER_DIL):
            bias = -slopes[g * N_HEADS_PER_DIL + jh] * dist
            tabs.append(np.where(np.abs(rel) <= n, bias, np.float32(NEG_INF)))
    return jnp.asarray(np.stack(tabs).astype(np.float32))


def _dil_body(*refs, seq_len):
    n_g = len(DIL_GROUPS)
    groups = [refs[7 * g:7 * g + 7] for g in range(n_g)]
    bias_refs = refs[7 * n_g:8 * n_g]
    o_ref, o_scr, l_scr = refs[8 * n_g:]
    i = pl.program_id(1)
    tq = o_ref.shape[2]
    halo = (DIL_K - DIL_Q) // 2

    for g, (_, d) in enumerate(DIL_GROUPS):
        q_ref, k_ref, kp_ref, kn_ref, v_ref, vp_ref, vn_ref = groups[g]
        n_rows = tq // d
        n_sub = seq_len // d
        bias = bias_refs[g][0]

        def window(cur, prv, nxt, e, lo):
            lanes = slice(e * HEAD_DIM, (e + 1) * HEAD_DIM)
            pieces = []
            if lo < 0:
                pieces.append(prv[0, 0, :, lanes])
            pieces.append(cur[0, 0, max(lo, 0):min(lo + DIL_K, n_rows), lanes])
            if lo + DIL_K > n_rows:
                pieces.append(nxt[0, 0, :, lanes])
            return jnp.concatenate(pieces, axis=0) if len(pieces) > 1 else pieces[0]

        for e in range(d):
            for c in range(n_rows // DIL_Q):
                q = q_ref[0, 0, c * DIL_Q:(c + 1) * DIL_Q, e * HEAD_DIM:(e + 1) * HEAD_DIM]
                k = window(k_ref, kp_ref, kn_ref, e, c * DIL_Q - halo)
                v = window(v_ref, vp_ref, vn_ref, e, c * DIL_Q - halo)
                s = _qk(q, k) * ATTN_SCALE + bias
                if c == 0 or c == n_rows // DIL_Q - 1:
                    kpos = i * (tq // d) + (c * DIL_Q - halo) + lax.broadcasted_iota(jnp.int32, (1, DIL_K), 1)
                    s = jnp.where((kpos >= 0) & (kpos < n_sub), s, NEG_INF)
                o, lse = _masked_softmax_pv(s, v)
                rows = pl.ds(c * DIL_Q * d + e, DIL_Q, stride=d) if d > 1 else pl.ds(c * DIL_Q, DIL_Q)
                o_scr[g, rows, :] = o
                l_scr[g, rows, :] = jnp.broadcast_to(lse, (DIL_Q, HEAD_DIM))

    rows = 256

    def merge(r, carry):
        r0 = pl.multiple_of(r * rows, rows)
        ls = [l_scr[g, pl.ds(r0, rows), :] for g in range(n_g)]
        m = functools.reduce(jnp.maximum, ls)
        ws = [jnp.exp(l - m) for l in ls]
        num = sum(w * o_scr[g, pl.ds(r0, rows), :] for g, w in enumerate(ws))
        o_ref[0, 0, pl.ds(r0, rows), :] = (num / sum(ws)).astype(o_ref.dtype)
        return carry
    lax.fori_loop(0, tq // rows, merge, 0)


def _dil_attn(z, tab):
    b, _, l, _ = z.shape
    tq = min(DIL_TQ, l)
    halo = (DIL_K - DIL_Q) // 2
    assert l % tq == 0 and tq % 2048 == 0

    def cur(h0, d):
        return pl.BlockSpec((1, 1, tq // d, d * HEAD_DIM), lambda bi, i, h: (bi, h0 + h, i, 0))

    def prev(h0, d):
        r = tq // d // halo
        return pl.BlockSpec((1, 1, halo, d * HEAD_DIM), lambda bi, i, h: (bi, h0 + h, jnp.maximum(i * r - 1, 0), 0))

    def nxt(h0, d):
        r = tq // d // halo
        nblk = l // d // halo
        return pl.BlockSpec((1, 1, halo, d * HEAD_DIM),
                            lambda bi, i, h: (bi, h0 + h, jnp.minimum((i + 1) * r, nblk - 1), 0))

    in_specs, views = [], []
    for g, (_, d) in enumerate(DIL_GROUPS):
        in_specs += [cur(ZH_Q_DL[g], d),
                     cur(ZH_K_DL[g], d), prev(ZH_K_DL[g], d), nxt(ZH_K_DL[g], d),
                     cur(ZH_V_DL[g], d), prev(ZH_V_DL[g], d), nxt(ZH_V_DL[g], d)]
        views += [z.reshape(b, Z_HEADS, l // d, d * HEAD_DIM)] * 7
    for g in range(len(DIL_GROUPS)):
        in_specs.append(pl.BlockSpec((1, DIL_Q, DIL_K), lambda bi, i, h, g=g: (g * N_HEADS_PER_DIL + h, 0, 0)))

    n_g = len(DIL_GROUPS)
    return pl.pallas_call(
        functools.partial(_dil_body, seq_len=l),
        grid=(b, l // tq, N_HEADS_PER_DIL),
        in_specs=in_specs,
        out_specs=pl.BlockSpec((1, 1, tq, HEAD_DIM), lambda bi, i, h: (bi, h, i, 0)),
        out_shape=jax.ShapeDtypeStruct((b, N_HEADS_PER_DIL, l, HEAD_DIM), BF16),
        scratch_shapes=[pltpu.VMEM((n_g, tq, HEAD_DIM), F32), pltpu.VMEM((n_g, tq, HEAD_DIM), F32)],
        compiler_params=_params("parallel", "parallel", "parallel"),
        name="dil_attn",
    )(*views, *([tab] * n_g))


def _out_proj_body(ona_ref, odl_ref, gna_ref, gdl_ref, x_ref, gt_ref, wna_ref, wdl_ref, wo_ref, o_ref):
    def heads(ref):
        return jnp.concatenate([ref[0, h] for h in range(ref.shape[1])], axis=-1)

    y_na = jnp.dot(heads(ona_ref), wna_ref[...], preferred_element_type=F32)
    y_dl = jnp.dot(heads(odl_ref), wdl_ref[...], preferred_element_type=F32)
    merged = jax.nn.sigmoid(heads(gna_ref).astype(F32)) * y_na + jax.nn.sigmoid(heads(gdl_ref).astype(F32)) * y_dl
    mix = jnp.dot(merged.astype(BF16), wo_ref[...], preferred_element_type=F32)
    o_ref[0] = x_ref[0] + gt_ref[0] * mix


def _out_proj(o_na, o_dl, z, x, gt, w_na_out, w_dil_out, w_o):
    b, l, d = x.shape
    tm = min(OUT_TM, l)
    n_gate_heads = d // HEAD_DIM
    assert ZH_G_NA == 0 and ZH_G_DL == n_gate_heads

    def const(shape):
        return pl.BlockSpec(shape, lambda bi, i: (0, 0), pipeline_mode=pl.Buffered(1))

    return pl.pallas_call(
        _out_proj_body,
        grid=(b, l // tm),
        in_specs=[pl.BlockSpec((1, N_HEADS_NA, tm, HEAD_DIM), lambda bi, i: (bi, 0, i, 0)),
                  pl.BlockSpec((1, N_HEADS_PER_DIL, tm, HEAD_DIM), lambda bi, i: (bi, 0, i, 0)),
                  pl.BlockSpec((1, n_gate_heads, tm, HEAD_DIM), lambda bi, i: (bi, 0, i, 0)),
                  pl.BlockSpec((1, n_gate_heads, tm, HEAD_DIM), lambda bi, i: (bi, 1, i, 0)),
                  pl.BlockSpec((1, tm, d), lambda bi, i: (bi, i, 0)),
                  pl.BlockSpec((1, 1, d), lambda bi, i: (bi, 0, 0)),
                  const(w_na_out.shape), const(w_dil_out.shape), const(w_o.shape)],
        out_specs=pl.BlockSpec((1, tm, d), lambda bi, i: (bi, i, 0)),
        out_shape=jax.ShapeDtypeStruct((b, l, d), F32),
        compiler_params=_params("parallel", "parallel"),
        name="out_proj",
    )(o_na, o_dl, z, z, x, gt, w_na_out, w_dil_out, w_o)


def _ffn_body(x_ref, sh_ref, sc_ref, gt_ref, g_ref, gf_ref, w1_ref, w2_ref, o_ref, u_ref):
    f = pl.program_id(2)
    tm, d = u_ref.shape
    rows = NORM_ROWS

    @pl.when(f == 0)
    def _():
        def zero_acc(r0):
            o_ref[0, pl.ds(r0, rows), :] = jnp.zeros((rows, d), F32)
        _modulated_norm_rows(x_ref, u_ref, g_ref[...], sc_ref[0], sh_ref[0], on_chunk=zero_acc)

    h = jnp.dot(u_ref[...], w1_ref[0], preferred_element_type=F32)
    h = jnp.square(jnp.maximum(h, 0.0)).astype(BF16)
    for n0 in range(0, d, FFN_TN):
        o_ref[0, :, n0:n0 + FFN_TN] += jnp.dot(h, w2_ref[:, n0:n0 + FFN_TN], preferred_element_type=F32)

    @pl.when(f == pl.num_programs(2) - 1)
    def _():
        def chunk(r, carry):
            r0 = pl.multiple_of(r * rows, rows)
            x2 = x_ref[0, pl.ds(r0, rows), :] + gt_ref[0] * o_ref[0, pl.ds(r0, rows), :]
            o_ref[0, pl.ds(r0, rows), :] = x2
            rinv = lax.rsqrt(jnp.mean(x2 * x2, axis=-1, keepdims=True) + NORM_EPS)
            o_ref[0, pl.ds(r0, rows), :] = o_ref[0, pl.ds(r0, rows), :] * rinv * gf_ref[...]
            return carry
        lax.fori_loop(0, tm // rows, chunk, 0, unroll=2)


def _ffn(x, sh, sc, gt, g_ffn, g_final, w_ff1, w_ff2):
    b, l, d = x.shape
    tm = min(FFN_TM, l)
    n_f, _, tf = w_ff1.shape
    return pl.pallas_call(
        _ffn_body,
        grid=(b, l // tm, n_f),
        in_specs=[pl.BlockSpec((1, tm, d), lambda bi, i, f: (bi, i, 0)),
                  pl.BlockSpec((1, 1, d), lambda bi, i, f: (bi, 0, 0)),
                  pl.BlockSpec((1, 1, d), lambda bi, i, f: (bi, 0, 0)),
                  pl.BlockSpec((1, 1, d), lambda bi, i, f: (bi, 0, 0)),
                  pl.BlockSpec((1, d), lambda bi, i, f: (0, 0)),
                  pl.BlockSpec((1, d), lambda bi, i, f: (0, 0)),
                  pl.BlockSpec((1, d, tf), lambda bi, i, f: (f, 0, 0)),
                  pl.BlockSpec((tf, d), lambda bi, i, f: (f, 0))],
        out_specs=pl.BlockSpec((1, tm, d), lambda bi, i, f: (bi, i, 0)),
        out_shape=jax.ShapeDtypeStruct((b, l, d), F32),
        scratch_shapes=[pltpu.VMEM((tm, d), BF16)],
        compiler_params=_params("parallel", "parallel", "arbitrary"),
        name="ffn",
    )(x, sh, sc, gt, g_ffn, g_final, w_ff1, w_ff2)


def _layer(x, mod, g_mix, w_in_r, na_tab, dil_tab, w_na_out, w_dil_out, w_o, g_ffn, w_ff1, w_ff2, g_final):
    d = x.shape[-1]
    sh1, sc1, gt1, sh2, sc2, gt2 = [mod[:, None, k * d:(k + 1) * d] for k in range(6)]
    z = _in_proj(x, sh1, sc1, g_mix, w_in_r)
    o_na = _na_attn(z, na_tab)
    o_dl = _dil_attn(z, dil_tab)
    x1 = _out_proj(o_na, o_dl, z, x, gt1, w_na_out, w_dil_out, w_o)
    return _ffn(x1, sh2, sc2, gt2, g_ffn, g_final, w_ff1, w_ff2)


def kernel(x_prompt, x_sample, c_prompt, c_sample, w_mod, b_mod, g_mix, w_in, rpb, w_na_out, w_dil_out, w_o,
           g_ffn, w_ff1, w_ff2, g_final):
    assert w_mod.shape[0] == 1, "single-layer trunk"
    d = x_prompt.shape[-1]
    nb_p, nb_s = c_prompt.shape[0], c_sample.shape[0]
    pad = (-(nb_p + nb_s)) % 8
    c_all = jnp.concatenate([c_prompt, c_sample, jnp.zeros((pad, d), F32)], axis=0)
    mod = _modulation(c_all, w_mod[0], b_mod[0])

    def tile_major(w, tn):
        return w.reshape(w.shape[0], w.shape[1] // tn, tn).transpose(1, 0, 2)

    w_in_r = jnp.concatenate([w_in[0][:, c0:c0 + n] for c0, n in _w_in_column_order(d)], axis=1).astype(BF16)
    shared = (g_mix[0][None], tile_major(w_in_r, IN_TN), _na_bias_tables(rpb[0]), _dil_bias_tables(),
              w_na_out[0].astype(BF16), w_dil_out[0].astype(BF16), w_o[0].astype(BF16),
              g_ffn[0][None], tile_major(w_ff1[0].astype(BF16), FFN_TF), w_ff2[0].astype(BF16), g_final[None])
    y_prompt = _layer(x_prompt, mod[:nb_p], *shared)
    y_sample = _layer(x_sample, mod[nb_p:nb_p + nb_s], *shared)
    return (y_prompt, y_sample)
```

```python
import functools

import numpy as np
import jax
import jax.numpy as jnp
from jax import lax
from jax.experimental import pallas as pl
from jax.experimental.pallas import tpu as pltpu

F32 = jnp.float32
BF16 = jnp.bfloat16

HEAD_DIM = 128
N_HEADS_NA = 4
NA_ROWS = 8
NA_KW = 16
GRID_W = 64
DIL_GROUPS = ((128, 1), (512, 4), (2048, 16))
N_HEADS_PER_DIL = 4
N_HEADS_DIL = N_HEADS_PER_DIL * len(DIL_GROUPS)
ALIBI_MAX_BIAS = 8.0
NORM_EPS = 1e-6
NEG_INF = -1e30
ATTN_SCALE = HEAD_DIM ** -0.5
LOG2E = float(np.log2(np.e))
LN2 = float(np.log(2.0))
Q_PRESCALE = ATTN_SCALE * LOG2E

VMEM_LIMIT_BYTES = 60 * 1024 * 1024

ZH_G_NA, ZH_G_DL = 0, 16
ZH_Q_NA, ZH_K_NA, ZH_V_NA = 36, 40, 44
ZH_Q_DL = (52, 68, 32)
ZH_K_DL = (56, 72, 48)
ZH_V_DL = (60, 76, 64)
Z_HEADS = 80

IN_TM = 1024
IN_TN = 2048
IN_SUB = 512

NA_TQ = 1024
NA_GROUP = 4 * GRID_W
NA_KEYS = 12 * GRID_W
DIL_TQ = 2048
DIL_Q = 128
DIL_K = 256
RES_ROWS = (DIL_K - DIL_Q) // 2
OUT_TM = 512
NORM_ROWS = 64
FFN_TM = 1024
FFN_TF = 512
FFN_TN = 512


def _params(*sem):
    return pltpu.CompilerParams(dimension_semantics=sem, vmem_limit_bytes=VMEM_LIMIT_BYTES)


def _w_in_column_order(d_model):
    d_na = N_HEADS_NA * HEAD_DIM
    d_dil = N_HEADS_DIL * HEAD_DIM
    d_grp = N_HEADS_PER_DIL * HEAD_DIM
    q_na, k_na, v_na = 0, d_na, 2 * d_na
    q_dl, k_dl, v_dl = 3 * d_na, 3 * d_na + d_dil, 3 * d_na + 2 * d_dil
    g_na = 3 * d_na + 3 * d_dil
    g_dl = g_na + d_model
    by_head = {ZH_G_NA: (g_na, d_model), ZH_G_DL: (g_dl, d_model),
               ZH_Q_NA: (q_na, d_na), ZH_K_NA: (k_na, d_na), ZH_V_NA: (v_na, d_na)}
    for g in range(len(DIL_GROUPS)):
        by_head[ZH_Q_DL[g]] = (q_dl + g * d_grp, d_grp)
        by_head[ZH_K_DL[g]] = (k_dl + g * d_grp, d_grp)
        by_head[ZH_V_DL[g]] = (v_dl + g * d_grp, d_grp)
    order, head = [], 0
    for h0 in sorted(by_head):
        assert h0 == head, "head-major layout must be contiguous"
        order.append(by_head[h0])
        head += by_head[h0][1] // HEAD_DIM
    assert head == Z_HEADS
    return order


def _in_tile_patterns():
    comp = {ZH_Q_NA: (1, True), ZH_K_NA: (1, False), ZH_V_NA: (1, False)}
    for g, (_, d) in enumerate(DIL_GROUPS):
        comp.update({ZH_Q_DL[g]: (d, True), ZH_K_DL[g]: (d, False), ZH_V_DL[g]: (d, False)})
    tile_heads, sub_heads = IN_TN // HEAD_DIM, IN_SUB // HEAD_DIM
    return tuple(tuple(comp.get(t * tile_heads + s * sub_heads, (1, False)) for s in range(IN_TN // IN_SUB))
                 for t in range(Z_HEADS // tile_heads))


def _w_in_tiles(w_in, d_model):
    tiles, parts, width = [], [], 0
    for c0, n in _w_in_column_order(d_model):
        while n > 0:
            take = min(n, IN_TN - width)
            parts.append(w_in[:, c0:c0 + take])
            c0, n, width = c0 + take, n - take, width + take
            if width == IN_TN:
                tiles.append(jnp.concatenate(parts, axis=1))
                parts, width = [], 0
    assert not parts
    return jnp.stack(tiles).astype(BF16)


def _mod_body(c_ref, w_ref, b_ref, o_ref):
    c = c_ref[...]
    s = (c * jax.nn.sigmoid(c)).astype(BF16)
    o_ref[...] = jnp.dot(s, w_ref[...].astype(BF16), preferred_element_type=F32) + b_ref[...]


def _modulation(c, w_mod, b_mod):
    rows, d = c.shape
    n = w_mod.shape[1]
    tn = 1024
    return pl.pallas_call(
        _mod_body,
        grid=(n // tn,),
        in_specs=[pl.BlockSpec((rows, d), lambda j: (0, 0)),
                  pl.BlockSpec((d, tn), lambda j: (0, j)),
                  pl.BlockSpec((1, tn), lambda j: (0, j))],
        out_specs=pl.BlockSpec((rows, tn), lambda j: (0, j)),
        out_shape=jax.ShapeDtypeStruct((rows, n), F32),
        compiler_params=_params("parallel"),
        name="modulation",
    )(c, w_mod, b_mod.reshape(1, n))


def _modulated_norm_rows(x_ref, u_ref, g, sc, sh, on_chunk=None):
    tm = u_ref.shape[0]
    gs = g * (1.0 + sc)

    def chunk(r, carry):
        r0 = pl.multiple_of(r * NORM_ROWS, NORM_ROWS)
        x = x_ref[0, pl.ds(r0, NORM_ROWS), :]
        rinv = lax.rsqrt(jnp.mean(x * x, axis=-1, keepdims=True) + NORM_EPS)
        u_ref[pl.ds(r0, NORM_ROWS), :] = (x_ref[0, pl.ds(r0, NORM_ROWS), :] * rinv * gs + sh).astype(BF16)
        if on_chunk is not None:
            on_chunk(r0)
        return carry
    lax.fori_loop(0, tm // NORM_ROWS, chunk, 0, unroll=2)


def _masked_softmax_pv(s2, v):
    m = jnp.max(s2, axis=-1, keepdims=True)
    p = jnp.exp2(s2 - m)
    l = jnp.sum(p, axis=-1, keepdims=True)
    o = jnp.dot(p.astype(BF16), v, preferred_element_type=F32) / l
    return o, (m + jnp.log2(l)) * LN2


def _qk(q, k):
    return lax.dot_general(q, k, (((1,), (1,)), ((), ())), preferred_element_type=F32)


def _in_proj_body(x_ref, sh_ref, sc_ref, g_ref, w_ref, z_ref, u_ref, p_ref):
    j = pl.program_id(2)
    tm = u_ref.shape[0]
    heads_per_dot = IN_SUB // HEAD_DIM

    @pl.when(j == 0)
    def _():
        _modulated_norm_rows(x_ref, u_ref, g_ref[...], sc_ref[0], sh_ref[0])

    def subtile(s, d, is_query):
        acc = jnp.dot(u_ref[...], w_ref[0, :, s * IN_SUB:(s + 1) * IN_SUB], preferred_element_type=F32)
        if is_query:
            acc = acc * Q_PRESCALE
        for hh in range(heads_per_dot):
            a = acc[:, hh * HEAD_DIM:(hh + 1) * HEAD_DIM]
            zh = s * heads_per_dot + hh
            if d == 1:
                z_ref[0, zh] = a.astype(BF16)
                continue
            p_ref[hh] = a
            for blk in range(tm // (RES_ROWS * d)):
                for e in range(d):
                    r0 = (blk * d + e) * RES_ROWS
                    src = pl.ds(blk * RES_ROWS * d + e, RES_ROWS, stride=d)
                    z_ref[0, zh, r0:r0 + RES_ROWS, :] = p_ref[hh, src, :].astype(BF16)

    patterns = _in_tile_patterns()
    for pattern in sorted(set(patterns)):
        tiles = [t for t, k in enumerate(patterns) if k == pattern]
        cond = functools.reduce(jnp.logical_or, [j == t for t in tiles])

        @pl.when(cond)
        def _(pattern=pattern):
            for s, (d, is_query) in enumerate(pattern):
                subtile(s, d, is_query)


def _in_proj(x, sh, sc, g, w_in_r):
    b, l, d = x.shape
    tm = min(IN_TM, l)
    n_tiles = w_in_r.shape[0]
    assert l % tm == 0 and w_in_r.shape[1:] == (d, IN_TN) and n_tiles == len(_in_tile_patterns())
    assert all(tm % (RES_ROWS * dd) == 0 for pat in _in_tile_patterns() for dd, _ in pat)
    return pl.pallas_call(
        _in_proj_body,
        grid=(b, l // tm, n_tiles),
        in_specs=[pl.BlockSpec((1, tm, d), lambda bi, i, j: (bi, i, 0)),
                  pl.BlockSpec((1, 1, d), lambda bi, i, j: (bi, 0, 0)),
                  pl.BlockSpec((1, 1, d), lambda bi, i, j: (bi, 0, 0)),
                  pl.BlockSpec((1, d), lambda bi, i, j: (0, 0)),
                  pl.BlockSpec((1, d, IN_TN), lambda bi, i, j: (j, 0, 0))],
        out_specs=pl.BlockSpec((1, IN_TN // HEAD_DIM, tm, HEAD_DIM), lambda bi, i, j: (bi, j, i, 0)),
        out_shape=jax.ShapeDtypeStruct((b, Z_HEADS, l, HEAD_DIM), BF16),
        scratch_shapes=[pltpu.VMEM((tm, d), BF16), pltpu.VMEM((IN_SUB // HEAD_DIM, tm, HEAD_DIM), F32)],
        compiler_params=_params("parallel", "parallel", "arbitrary"),
        name="in_proj",
    )(x, sh, sc, g, w_in_r)


def _na_bias_tables(rpb):
    rho = np.arange(4)[:, None, None, None]
    qc = np.arange(GRID_W)[None, :, None, None]
    f = np.arange(12)[None, None, :, None]
    kc = np.arange(GRID_W)[None, None, None, :]
    col_start = np.clip(qc - NA_KW // 2, 0, GRID_W - NA_KW)
    col_ok = (kc >= col_start) & (kc < col_start + NA_KW)
    row_ok = np.stack([
        (f >= rho) & (f < rho + NA_ROWS),
        (f >= 4) & (f < 4 + NA_ROWS) & (rho >= 0),
        (f >= 0) & (f < NA_ROWS) & (rho >= 0),
    ])
    ok = row_ok & col_ok[None]
    padded = jnp.pad(rpb.astype(F32), ((0, 0), (0, 0), (GRID_W, GRID_W)))
    c0 = GRID_W + NA_KW - 1
    col = jnp.stack([padded[:, :, c0 - q:c0 - q + GRID_W] for q in range(GRID_W)], axis=2)
    bias = jnp.stack([jnp.stack([col[:, fr - r + 3] for fr in range(12)], axis=2) for r in range(4)], axis=1)
    tab = jnp.where(ok[:, None], bias[None] * LOG2E, NEG_INF)
    return tab.reshape(3, rpb.shape[0], NA_GROUP, NA_KEYS)


def _na_body(q_ref, k_ref, kp_ref, kn_ref, v_ref, vp_ref, vn_ref, tab_ref, o_ref):
    i = pl.program_id(2)
    tq = q_ref.shape[2]
    n_grp = tq // NA_GROUP

    def window(cur, prv, nxt, a):
        pieces = []
        for ga in (a - 1, a, a + 1):
            if ga < 0:
                pieces.append(prv[0, 0])
            elif ga >= n_grp:
                pieces.append(nxt[0, 0])
            else:
                pieces.append(cur[0, 0, ga * NA_GROUP:(ga + 1) * NA_GROUP, :])
        return jnp.concatenate(pieces, axis=0)

    for a in range(n_grp):
        variant = 0
        if a == 0:
            variant = jnp.where(i == 0, 1, variant)
        if a == n_grp - 1:
            variant = jnp.where(i == pl.num_programs(2) - 1, 2, variant)
        q = q_ref[0, 0, a * NA_GROUP:(a + 1) * NA_GROUP, :]
        s = _qk(q, window(k_ref, kp_ref, kn_ref, a)) + tab_ref[variant, 0]
        o, _ = _masked_softmax_pv(s, window(v_ref, vp_ref, vn_ref, a))
        o_ref[0, 0, a * NA_GROUP:(a + 1) * NA_GROUP, :] = o.astype(o_ref.dtype)


def _na_attn(z, tab):
    b, _, l, _ = z.shape
    tq = min(NA_TQ, l)
    halo = NA_GROUP
    n_halo_blocks = l // halo
    r = tq // halo
    assert l % tq == 0 and l // NA_GROUP >= 3

    def cur(h0):
        return pl.BlockSpec((1, 1, tq, HEAD_DIM), lambda bi, h, i: (bi, h0 + h, i, 0))

    def prev(h0):
        return pl.BlockSpec((1, 1, halo, HEAD_DIM), lambda bi, h, i: (bi, h0 + h, jnp.maximum(i * r - 1, 0), 0))

    def nxt(h0):
        return pl.BlockSpec((1, 1, halo, HEAD_DIM),
                            lambda bi, h, i: (bi, h0 + h, jnp.minimum((i + 1) * r, n_halo_blocks - 1), 0))

    return pl.pallas_call(
        _na_body,
        grid=(b, N_HEADS_NA, l // tq),
        in_specs=[cur(ZH_Q_NA), cur(ZH_K_NA), prev(ZH_K_NA), nxt(ZH_K_NA),
                  cur(ZH_V_NA), prev(ZH_V_NA), nxt(ZH_V_NA),
                  pl.BlockSpec((3, 1, NA_GROUP, NA_KEYS), lambda bi, h, i: (0, h, 0, 0))],
        out_specs=pl.BlockSpec((1, 1, tq, HEAD_DIM), lambda bi, h, i: (bi, h, i, 0)),
        out_shape=jax.ShapeDtypeStruct((b, N_HEADS_NA, l, HEAD_DIM), BF16),
        compiler_params=_params("parallel", "parallel", "parallel"),
        name="na_attn",
    )(z, z, z, z, z, z, z, tab)


def _dil_bias_tables():
    h = np.arange(1, N_HEADS_DIL + 1, dtype=np.float32)
    slopes = np.power(np.float32(2.0), -ALIBI_MAX_BIAS * h / N_HEADS_DIL).astype(np.float32)
    rel = np.arange(DIL_K)[None, :] - (DIL_K - DIL_Q) // 2 - np.arange(DIL_Q)[:, None]
    tabs = []
    for g, (window, dilation) in enumerate(DIL_GROUPS):
        n = window // (2 * dilation)
        assert n == (DIL_K - DIL_Q) // 2
        dist = (np.abs(rel) * dilation).astype(np.float32)
        for jh in range(N_HEADS_PER_DIL):
            bias = -slopes[g * N_HEADS_PER_DIL + jh] * dist
            tabs.append(np.where(np.abs(rel) <= n, bias * np.float32(LOG2E), np.float32(NEG_INF)))
    return jnp.asarray(np.stack(tabs).astype(np.float32))


def _dil_body(*refs, seq_len):
    n_g = len(DIL_GROUPS)
    groups = [refs[7 * g:7 * g + 7] for g in range(n_g)]
    bias_refs = refs[7 * n_g:8 * n_g]
    o_ref, o_scr, l_scr = refs[8 * n_g:]
    i = pl.program_id(1)
    tq = o_ref.shape[2]
    halo = (DIL_K - DIL_Q) // 2

    for g, (_, d) in enumerate(DIL_GROUPS):
        q_ref, k_ref, kp_ref, kn_ref, v_ref, vp_ref, vn_ref = groups[g]
        n_piece = tq // (halo * d)
        n_sub = seq_len // d
        bias = bias_refs[g][0]

        def piece(cur, prv, nxt, e, j):
            if j < 0:
                return prv[0, 0, e * halo:(e + 1) * halo, :]
            if j >= n_piece:
                return nxt[0, 0, e * halo:(e + 1) * halo, :]
            r0 = (j * d + e) * halo
            return cur[0, 0, r0:r0 + halo, :]

        for e in range(d):
            for c in range(n_piece // 2):
                q = jnp.concatenate([piece(q_ref, None, None, e, j) for j in (2 * c, 2 * c + 1)], axis=0)
                k = jnp.concatenate([piece(k_ref, kp_ref, kn_ref, e, j) for j in range(2 * c - 1, 2 * c + 3)], axis=0)
                v = jnp.concatenate([piece(v_ref, vp_ref, vn_ref, e, j) for j in range(2 * c - 1, 2 * c + 3)], axis=0)
                s = _qk(q, k) + bias
                if c == 0 or c == n_piece // 2 - 1:
                    kpos = i * (tq // d) + (c * DIL_Q - halo) + lax.broadcasted_iota(jnp.int32, (1, DIL_K), 1)
                    s = jnp.where((kpos >= 0) & (kpos < n_sub), s, NEG_INF)
                o, lse = _masked_softmax_pv(s, v)
                rows = pl.ds(c * DIL_Q * d + e, DIL_Q, stride=d) if d > 1 else pl.ds(c * DIL_Q, DIL_Q)
                o_scr[g, rows, :] = o
                l_scr[g, rows, :] = jnp.broadcast_to(lse, (DIL_Q, HEAD_DIM))

    rows = 256

    def merge(r, carry):
        r0 = pl.multiple_of(r * rows, rows)
        ls = [l_scr[g, pl.ds(r0, rows), :] for g in range(n_g)]
        m = functools.reduce(jnp.maximum, ls)
        ws = [jnp.exp(l - m) for l in ls]
        num = sum(w * o_scr[g, pl.ds(r0, rows), :] for g, w in enumerate(ws))
        o_ref[0, 0, pl.ds(r0, rows), :] = (num / sum(ws)).astype(o_ref.dtype)
        return carry
    lax.fori_loop(0, tq // rows, merge, 0)


def _dil_attn(z, tab):
    b, _, l, _ = z.shape
    tq = min(DIL_TQ, l)
    halo = (DIL_K - DIL_Q) // 2
    assert l % tq == 0 and tq % 2048 == 0

    def cur(h0):
        return pl.BlockSpec((1, 1, tq, HEAD_DIM), lambda bi, i, h: (bi, h0 + h, i, 0))

    def prev(h0, rows):
        r = tq // rows
        return pl.BlockSpec((1, 1, rows, HEAD_DIM), lambda bi, i, h: (bi, h0 + h, jnp.maximum(i * r - 1, 0), 0))

    def nxt(h0, rows):
        r = tq // rows
        nblk = l // rows
        return pl.BlockSpec((1, 1, rows, HEAD_DIM),
                            lambda bi, i, h: (bi, h0 + h, jnp.minimum((i + 1) * r, nblk - 1), 0))

    in_specs = []
    for g, (_, d) in enumerate(DIL_GROUPS):
        hrows = halo * d
        in_specs += [cur(ZH_Q_DL[g]),
                     cur(ZH_K_DL[g]), prev(ZH_K_DL[g], hrows), nxt(ZH_K_DL[g], hrows),
                     cur(ZH_V_DL[g]), prev(ZH_V_DL[g], hrows), nxt(ZH_V_DL[g], hrows)]
    for g in range(len(DIL_GROUPS)):
        in_specs.append(pl.BlockSpec((1, DIL_Q, DIL_K), lambda bi, i, h, g=g: (g * N_HEADS_PER_DIL + h, 0, 0)))

    n_g = len(DIL_GROUPS)
    return pl.pallas_call(
        functools.partial(_dil_body, seq_len=l),
        grid=(b, l // tq, N_HEADS_PER_DIL),
        in_specs=in_specs,
        out_specs=pl.BlockSpec((1, 1, tq, HEAD_DIM), lambda bi, i, h: (bi, h, i, 0)),
        out_shape=jax.ShapeDtypeStruct((b, N_HEADS_PER_DIL, l, HEAD_DIM), BF16),
        scratch_shapes=[pltpu.VMEM((n_g, tq, HEAD_DIM), F32), pltpu.VMEM((n_g, tq, HEAD_DIM), F32)],
        compiler_params=_params("parallel", "parallel", "parallel"),
        name="dil_attn",
    )(*([z] * (7 * n_g)), *([tab] * n_g))


def _out_proj_body(ona_ref, odl_ref, gna_ref, gdl_ref, x_ref, gt_ref, wna_ref, wdl_ref, wo_ref, o_ref):
    def heads(ref):
        return jnp.concatenate([ref[0, h] for h in range(ref.shape[1])], axis=-1)

    y_na = jnp.dot(heads(ona_ref), wna_ref[...], preferred_element_type=F32)
    y_dl = jnp.dot(heads(odl_ref), wdl_ref[...], preferred_element_type=F32)
    merged = jax.nn.sigmoid(heads(gna_ref).astype(F32)) * y_na + jax.nn.sigmoid(heads(gdl_ref).astype(F32)) * y_dl
    mix = jnp.dot(merged.astype(BF16), wo_ref[...], preferred_element_type=F32)
    o_ref[0] = x_ref[0] + gt_ref[0] * mix


def _out_proj(o_na, o_dl, z, x, gt, w_na_out, w_dil_out, w_o):
    b, l, d = x.shape
    tm = min(OUT_TM, l)
    n_gate_heads = d // HEAD_DIM
    assert ZH_G_NA == 0 and ZH_G_DL == n_gate_heads

    def const(shape):
        return pl.BlockSpec(shape, lambda bi, i: (0, 0), pipeline_mode=pl.Buffered(1))

    return pl.pallas_call(
        _out_proj_body,
        grid=(b, l // tm),
        in_specs=[pl.BlockSpec((1, N_HEADS_NA, tm, HEAD_DIM), lambda bi, i: (bi, 0, i, 0)),
                  pl.BlockSpec((1, N_HEADS_PER_DIL, tm, HEAD_DIM), lambda bi, i: (bi, 0, i, 0)),
                  pl.BlockSpec((1, n_gate_heads, tm, HEAD_DIM), lambda bi, i: (bi, 0, i, 0)),
                  pl.BlockSpec((1, n_gate_heads, tm, HEAD_DIM), lambda bi, i: (bi, 1, i, 0)),
                  pl.BlockSpec((1, tm, d), lambda bi, i: (bi, i, 0)),
                  pl.BlockSpec((1, 1, d), lambda bi, i: (bi, 0, 0)),
                  const(w_na_out.shape), const(w_dil_out.shape), const(w_o.shape)],
        out_specs=pl.BlockSpec((1, tm, d), lambda bi, i: (bi, i, 0)),
        out_shape=jax.ShapeDtypeStruct((b, l, d), F32),
        compiler_params=_params("parallel", "parallel"),
        name="out_proj",
    )(o_na, o_dl, z, z, x, gt, w_na_out, w_dil_out, w_o)


def _ffn_body(x_ref, sh_ref, sc_ref, gt_ref, g_ref, gf_ref, w1_ref, w2_ref, o_ref, u_ref):
    f = pl.program_id(2)
    tm, d = u_ref.shape
    rows = NORM_ROWS

    @pl.when(f == 0)
    def _():
        def zero_acc(r0):
            o_ref[0, pl.ds(r0, rows), :] = jnp.zeros((rows, d), F32)
        _modulated_norm_rows(x_ref, u_ref, g_ref[...], sc_ref[0], sh_ref[0], on_chunk=zero_acc)

    h = jnp.dot(u_ref[...], w1_ref[...], preferred_element_type=F32)
    h = jnp.square(jnp.maximum(h, 0.0)).astype(BF16)
    for n0 in range(0, d, FFN_TN):
        o_ref[0, :, n0:n0 + FFN_TN] += jnp.dot(h, w2_ref[:, n0:n0 + FFN_TN], preferred_element_type=F32)

    @pl.when(f == pl.num_programs(2) - 1)
    def _():
        def chunk(r, carry):
            r0 = pl.multiple_of(r * rows, rows)
            x2 = x_ref[0, pl.ds(r0, rows), :] + gt_ref[0] * o_ref[0, pl.ds(r0, rows), :]
            o_ref[0, pl.ds(r0, rows), :] = x2
            rinv = lax.rsqrt(jnp.mean(x2 * x2, axis=-1, keepdims=True) + NORM_EPS)
            o_ref[0, pl.ds(r0, rows), :] = o_ref[0, pl.ds(r0, rows), :] * rinv * gf_ref[...]
            return carry
        lax.fori_loop(0, tm // rows, chunk, 0, unroll=2)


def _ffn(x, sh, sc, gt, g_ffn, g_final, w_ff1, w_ff2):
    b, l, d = x.shape
    tm = min(FFN_TM, l)
    tf = FFN_TF
    return pl.pallas_call(
        _ffn_body,
        grid=(b, l // tm, w_ff1.shape[1] // tf),
        in_specs=[pl.BlockSpec((1, tm, d), lambda bi, i, f: (bi, i, 0)),
                  pl.BlockSpec((1, 1, d), lambda bi, i, f: (bi, 0, 0)),
                  pl.BlockSpec((1, 1, d), lambda bi, i, f: (bi, 0, 0)),
                  pl.BlockSpec((1, 1, d), lambda bi, i, f: (bi, 0, 0)),
                  pl.BlockSpec((1, d), lambda bi, i, f: (0, 0)),
                  pl.BlockSpec((1, d), lambda bi, i, f: (0, 0)),
                  pl.BlockSpec((d, tf), lambda bi, i, f: (0, f)),
                  pl.BlockSpec((tf, d), lambda bi, i, f: (f, 0))],
        out_specs=pl.BlockSpec((1, tm, d), lambda bi, i, f: (bi, i, 0)),
        out_shape=jax.ShapeDtypeStruct((b, l, d), F32),
        scratch_shapes=[pltpu.VMEM((tm, d), BF16)],
        compiler_params=_params("parallel", "parallel", "arbitrary"),
        name="ffn",
    )(x, sh, sc, gt, g_ffn, g_final, w_ff1, w_ff2)


def _layer(x, mod, g_mix, w_in_r, na_tab, dil_tab, w_na_out, w_dil_out, w_o, g_ffn, w_ff1, w_ff2, g_final):
    d = x.shape[-1]
    sh1, sc1, gt1, sh2, sc2, gt2 = [mod[:, None, k * d:(k + 1) * d] for k in range(6)]
    z = _in_proj(x, sh1, sc1, g_mix, w_in_r)
    o_na = _na_attn(z, na_tab)
    o_dl = _dil_attn(z, dil_tab)
    x1 = _out_proj(o_na, o_dl, z, x, gt1, w_na_out, w_dil_out, w_o)
    return _ffn(x1, sh2, sc2, gt2, g_ffn, g_final, w_ff1, w_ff2)


def kernel(x_prompt, x_sample, c_prompt, c_sample, w_mod, b_mod, g_mix, w_in, rpb, w_na_out, w_dil_out, w_o,
           g_ffn, w_ff1, w_ff2, g_final):
    assert w_mod.shape[0] == 1, "single-layer trunk"
    d = x_prompt.shape[-1]
    nb_p, nb_s = c_prompt.shape[0], c_sample.shape[0]
    pad = (-(nb_p + nb_s)) % 8
    c_all = jnp.concatenate([c_prompt, c_sample, jnp.zeros((pad, d), F32)], axis=0)
    mod = _modulation(c_all, w_mod[0], b_mod[0])

    shared = (g_mix[0][None], _w_in_tiles(w_in[0], d), _na_bias_tables(rpb[0]), _dil_bias_tables(),
              w_na_out[0].astype(BF16), w_dil_out[0].astype(BF16), w_o[0].astype(BF16),
              g_ffn[0][None], w_ff1[0].astype(BF16), w_ff2[0].astype(BF16), g_final[None])
    y_prompt = _layer(x_prompt, mod[:nb_p], *shared)
    y_sample = _layer(x_sample, mod[nb_p:nb_p + nb_s], *shared)
    return (y_prompt, y_sample)
```

```python
import functools

import numpy as np
import jax
import jax.numpy as jnp
from jax import lax
from jax.experimental import pallas as pl
from jax.experimental.pallas import tpu as pltpu

F32 = jnp.float32
BF16 = jnp.bfloat16

HEAD_DIM = 128
N_HEADS_NA = 4
NA_ROWS = 8
NA_KW = 16
GRID_W = 64
DIL_GROUPS = ((128, 1), (512, 4), (2048, 16))
N_HEADS_PER_DIL = 4
N_HEADS_DIL = N_HEADS_PER_DIL * len(DIL_GROUPS)
ALIBI_MAX_BIAS = 8.0
NORM_EPS = 1e-6
NEG_INF = -1e30
ATTN_SCALE = HEAD_DIM ** -0.5
LOG2E = float(np.log2(np.e))
LN2 = float(np.log(2.0))
Q_PRESCALE = ATTN_SCALE * LOG2E

VMEM_LIMIT_BYTES = 60 * 1024 * 1024

ZH_G_NA, ZH_G_DL = 0, 16
ZH_Q_NA, ZH_K_NA, ZH_V_NA = 36, 40, 44
ZH_Q_DL = (52, 68, 32)
ZH_K_DL = (56, 72, 48)
ZH_V_DL = (60, 76, 64)
Z_HEADS = 80

IN_TM = 1024
IN_TN = 2048
IN_SUB = 512

NA_TQ = 1024
NA_GROUP = 4 * GRID_W
NA_KEYS = 12 * GRID_W
DIL_TQ = 2048
DIL_Q = 128
DIL_K = 256
RES_ROWS = (DIL_K - DIL_Q) // 2
OUT_TM = 512
NORM_ROWS = 64
FFN_TM = 1024
FFN_TF = 512
FFN_TN = 512


def _params(*sem):
    return pltpu.CompilerParams(dimension_semantics=sem, vmem_limit_bytes=VMEM_LIMIT_BYTES)


def _w_in_column_order(d_model):
    d_na = N_HEADS_NA * HEAD_DIM
    d_dil = N_HEADS_DIL * HEAD_DIM
    d_grp = N_HEADS_PER_DIL * HEAD_DIM
    q_na, k_na, v_na = 0, d_na, 2 * d_na
    q_dl, k_dl, v_dl = 3 * d_na, 3 * d_na + d_dil, 3 * d_na + 2 * d_dil
    g_na = 3 * d_na + 3 * d_dil
    g_dl = g_na + d_model
    by_head = {ZH_G_NA: (g_na, d_model), ZH_G_DL: (g_dl, d_model),
               ZH_Q_NA: (q_na, d_na), ZH_K_NA: (k_na, d_na), ZH_V_NA: (v_na, d_na)}
    for g in range(len(DIL_GROUPS)):
        by_head[ZH_Q_DL[g]] = (q_dl + g * d_grp, d_grp)
        by_head[ZH_K_DL[g]] = (k_dl + g * d_grp, d_grp)
        by_head[ZH_V_DL[g]] = (v_dl + g * d_grp, d_grp)
    order, head = [], 0
    for h0 in sorted(by_head):
        assert h0 == head, "head-major layout must be contiguous"
        order.append(by_head[h0])
        head += by_head[h0][1] // HEAD_DIM
    assert head == Z_HEADS
    return order


def _in_tile_patterns():
    comp = {ZH_Q_NA: (1, True), ZH_K_NA: (1, False), ZH_V_NA: (1, False)}
    for g, (_, d) in enumerate(DIL_GROUPS):
        comp.update({ZH_Q_DL[g]: (d, True), ZH_K_DL[g]: (d, False), ZH_V_DL[g]: (d, False)})
    tile_heads, sub_heads = IN_TN // HEAD_DIM, IN_SUB // HEAD_DIM
    return tuple(tuple(comp.get(t * tile_heads + s * sub_heads, (1, False)) for s in range(IN_TN // IN_SUB))
                 for t in range(Z_HEADS // tile_heads))


def _w_in_tiles(w_in, d_model):
    tiles, parts, width = [], [], 0
    for c0, n in _w_in_column_order(d_model):
        while n > 0:
            take = min(n, IN_TN - width)
            parts.append(w_in[:, c0:c0 + take])
            c0, n, width = c0 + take, n - take, width + take
            if width == IN_TN:
                tiles.append(jnp.concatenate(parts, axis=1))
                parts, width = [], 0
    assert not parts
    return jnp.stack(tiles).astype(BF16)


def _mod_body(c_ref, w_ref, b_ref, o_ref):
    c = c_ref[...]
    s = (c * jax.nn.sigmoid(c)).astype(BF16)
    o_ref[...] = jnp.dot(s, w_ref[...].astype(BF16), preferred_element_type=F32) + b_ref[...]


def _modulation(c, w_mod, b_mod):
    rows, d = c.shape
    n = w_mod.shape[1]
    tn = 1024
    return pl.pallas_call(
        _mod_body,
        grid=(n // tn,),
        in_specs=[pl.BlockSpec((rows, d), lambda j: (0, 0)),
                  pl.BlockSpec((d, tn), lambda j: (0, j)),
                  pl.BlockSpec((1, tn), lambda j: (0, j))],
        out_specs=pl.BlockSpec((rows, tn), lambda j: (0, j)),
        out_shape=jax.ShapeDtypeStruct((rows, n), F32),
        compiler_params=_params("parallel"),
        name="modulation",
    )(c, w_mod, b_mod.reshape(1, n))


def _modulated_norm_rows(x_ref, u_ref, g, sc, sh, on_chunk=None):
    tm = u_ref.shape[0]
    gs = g * (1.0 + sc)

    def chunk(r, carry):
        r0 = pl.multiple_of(r * NORM_ROWS, NORM_ROWS)
        x = x_ref[0, pl.ds(r0, NORM_ROWS), :]
        rinv = lax.rsqrt(jnp.mean(x * x, axis=-1, keepdims=True) + NORM_EPS)
        u_ref[pl.ds(r0, NORM_ROWS), :] = (x_ref[0, pl.ds(r0, NORM_ROWS), :] * rinv * gs + sh).astype(BF16)
        if on_chunk is not None:
            on_chunk(r0)
        return carry
    lax.fori_loop(0, tm // NORM_ROWS, chunk, 0, unroll=2)


def _masked_softmax_pv(s2, v):
    m = jnp.max(s2, axis=-1, keepdims=True)
    p = jnp.exp2(s2 - m)
    l = jnp.sum(p, axis=-1, keepdims=True)
    o = jnp.dot(p.astype(BF16), v, preferred_element_type=F32) / l
    return o, (m + jnp.log2(l)) * LN2


def _qk(q, k):
    return lax.dot_general(q, k, (((1,), (1,)), ((), ())), preferred_element_type=F32)


def _in_proj_body(x_ref, sh_ref, sc_ref, g_ref, w_ref, z_ref, u_ref, p_ref, p2_ref):
    j = pl.program_id(2)
    tm = u_ref.shape[0]
    heads_per_dot = IN_SUB // HEAD_DIM

    @pl.when(j == 0)
    def _():
        _modulated_norm_rows(x_ref, u_ref, g_ref[...], sc_ref[0], sh_ref[0])

    def matmul(s):
        return jnp.dot(u_ref[...], w_ref[0, :, s * IN_SUB:(s + 1) * IN_SUB], preferred_element_type=F32)

    def write_out(s, acc, d, is_query):
        if is_query:
            acc = acc * Q_PRESCALE
        for hh in range(heads_per_dot):
            a = acc[:, hh * HEAD_DIM:(hh + 1) * HEAD_DIM]
            zh = s * heads_per_dot + hh
            if d == 1:
                z_ref[0, zh] = a.astype(BF16)
                continue
            p_ref[hh] = a
            src_ref = p_ref
            if d == 16:
                for e_lo in range(4):
                    p2_ref[hh, e_lo * (tm // 4):(e_lo + 1) * (tm // 4), :] = p_ref[hh, pl.ds(e_lo, tm // 4, stride=4), :]
                src_ref = p2_ref
            for blk in range(tm // (RES_ROWS * d)):
                for e in range(d):
                    r0 = (blk * d + e) * RES_ROWS
                    if d == 16:
                        src = pl.ds((e % 4) * (tm // 4) + blk * RES_ROWS * 4 + e // 4, RES_ROWS, stride=4)
                    else:
                        src = pl.ds(blk * RES_ROWS * d + e, RES_ROWS, stride=d)
                    z_ref[0, zh, r0:r0 + RES_ROWS, :] = src_ref[hh, src, :].astype(BF16)

    patterns = _in_tile_patterns()
    for pattern in sorted(set(patterns)):
        tiles = [t for t, k in enumerate(patterns) if k == pattern]
        cond = functools.reduce(jnp.logical_or, [j == t for t in tiles])

        @pl.when(cond)
        def _(pattern=pattern):
            pending = None
            for s, (d, is_query) in enumerate(pattern):
                acc = matmul(s)
                if pending is not None:
                    write_out(*pending)
                pending = (s, acc, d, is_query)
            write_out(*pending)


def _in_proj(x, sh, sc, g, w_in_r):
    b, l, d = x.shape
    tm = min(IN_TM, l)
    n_tiles = w_in_r.shape[0]
    assert l % tm == 0 and w_in_r.shape[1:] == (d, IN_TN) and n_tiles == len(_in_tile_patterns())
    assert all(tm % (RES_ROWS * dd) == 0 for pat in _in_tile_patterns() for dd, _ in pat)
    return pl.pallas_call(
        _in_proj_body,
        grid=(b, l // tm, n_tiles),
        in_specs=[pl.BlockSpec((1, tm, d), lambda bi, i, j: (bi, i, 0)),
                  pl.BlockSpec((1, 1, d), lambda bi, i, j: (bi, 0, 0)),
                  pl.BlockSpec((1, 1, d), lambda bi, i, j: (bi, 0, 0)),
                  pl.BlockSpec((1, d), lambda bi, i, j: (0, 0)),
                  pl.BlockSpec((1, d, IN_TN), lambda bi, i, j: (j, 0, 0))],
        out_specs=pl.BlockSpec((1, IN_TN // HEAD_DIM, tm, HEAD_DIM), lambda bi, i, j: (bi, j, i, 0)),
        out_shape=jax.ShapeDtypeStruct((b, Z_HEADS, l, HEAD_DIM), BF16),
        scratch_shapes=[pltpu.VMEM((tm, d), BF16)] + [pltpu.VMEM((IN_SUB // HEAD_DIM, tm, HEAD_DIM), F32)] * 2,
        compiler_params=_params("parallel", "parallel", "arbitrary"),
        name="in_proj",
    )(x, sh, sc, g, w_in_r)


def _na_bias_tables(rpb):
    rho = np.arange(4)[:, None, None, None]
    qc = np.arange(GRID_W)[None, :, None, None]
    f = np.arange(12)[None, None, :, None]
    kc = np.arange(GRID_W)[None, None, None, :]
    col_start = np.clip(qc - NA_KW // 2, 0, GRID_W - NA_KW)
    col_ok = (kc >= col_start) & (kc < col_start + NA_KW)
    row_ok = np.stack([
        (f >= rho) & (f < rho + NA_ROWS),
        (f >= 4) & (f < 4 + NA_ROWS) & (rho >= 0),
        (f >= 0) & (f < NA_ROWS) & (rho >= 0),
    ])
    ok = row_ok & col_ok[None]
    padded = jnp.pad(rpb.astype(F32), ((0, 0), (0, 0), (GRID_W, GRID_W)))
    c0 = GRID_W + NA_KW - 1
    col = jnp.stack([padded[:, :, c0 - q:c0 - q + GRID_W] for q in range(GRID_W)], axis=2)
    bias = jnp.stack([jnp.stack([col[:, fr - r + 3] for fr in range(12)], axis=2) for r in range(4)], axis=1)
    tab = jnp.where(ok[:, None], bias[None] * LOG2E, NEG_INF)
    return tab.reshape(3, rpb.shape[0], NA_GROUP, NA_KEYS)


def _na_body(q_ref, k_ref, kp_ref, kn_ref, v_ref, vp_ref, vn_ref, tab_ref, o_ref):
    i = pl.program_id(2)
    tq = q_ref.shape[2]
    n_grp = tq // NA_GROUP

    def window(cur, prv, nxt, a):
        pieces = []
        for ga in (a - 1, a, a + 1):
            if ga < 0:
                pieces.append(prv[0, 0])
            elif ga >= n_grp:
                pieces.append(nxt[0, 0])
            else:
                pieces.append(cur[0, 0, ga * NA_GROUP:(ga + 1) * NA_GROUP, :])
        return jnp.concatenate(pieces, axis=0)

    for a in range(n_grp):
        variant = 0
        if a == 0:
            variant = jnp.where(i == 0, 1, variant)
        if a == n_grp - 1:
            variant = jnp.where(i == pl.num_programs(2) - 1, 2, variant)
        q = q_ref[0, 0, a * NA_GROUP:(a + 1) * NA_GROUP, :]
        s = _qk(q, window(k_ref, kp_ref, kn_ref, a)) + tab_ref[variant, 0]
        o, _ = _masked_softmax_pv(s, window(v_ref, vp_ref, vn_ref, a))
        o_ref[0, 0, a * NA_GROUP:(a + 1) * NA_GROUP, :] = o.astype(o_ref.dtype)


def _na_attn(z, tab):
    b, _, l, _ = z.shape
    tq = min(NA_TQ, l)
    halo = NA_GROUP
    n_halo_blocks = l // halo
    r = tq // halo
    assert l % tq == 0 and l // NA_GROUP >= 3

    def cur(h0):
        return pl.BlockSpec((1, 1, tq, HEAD_DIM), lambda bi, h, i: (bi, h0 + h, i, 0))

    def prev(h0):
        return pl.BlockSpec((1, 1, halo, HEAD_DIM), lambda bi, h, i: (bi, h0 + h, jnp.maximum(i * r - 1, 0), 0))

    def nxt(h0):
        return pl.BlockSpec((1, 1, halo, HEAD_DIM),
                            lambda bi, h, i: (bi, h0 + h, jnp.minimum((i + 1) * r, n_halo_blocks - 1), 0))

    return pl.pallas_call(
        _na_body,
        grid=(b, N_HEADS_NA, l // tq),
        in_specs=[cur(ZH_Q_NA), cur(ZH_K_NA), prev(ZH_K_NA), nxt(ZH_K_NA),
                  cur(ZH_V_NA), prev(ZH_V_NA), nxt(ZH_V_NA),
                  pl.BlockSpec((3, 1, NA_GROUP, NA_KEYS), lambda bi, h, i: (0, h, 0, 0))],
        out_specs=pl.BlockSpec((1, 1, tq, HEAD_DIM), lambda bi, h, i: (bi, h, i, 0)),
        out_shape=jax.ShapeDtypeStruct((b, N_HEADS_NA, l, HEAD_DIM), BF16),
        compiler_params=_params("parallel", "parallel", "parallel"),
        name="na_attn",
    )(z, z, z, z, z, z, z, tab)


def _dil_bias_tables():
    h = np.arange(1, N_HEADS_DIL + 1, dtype=np.float32)
    slopes = np.power(np.float32(2.0), -ALIBI_MAX_BIAS * h / N_HEADS_DIL).astype(np.float32)
    rel = np.arange(DIL_K)[None, :] - (DIL_K - DIL_Q) // 2 - np.arange(DIL_Q)[:, None]
    tabs = []
    for g, (window, dilation) in enumerate(DIL_GROUPS):
        n = window // (2 * dilation)
        assert n == (DIL_K - DIL_Q) // 2
        dist = (np.abs(rel) * dilation).astype(np.float32)
        for jh in range(N_HEADS_PER_DIL):
            bias = -slopes[g * N_HEADS_PER_DIL + jh] * dist
            tabs.append(np.where(np.abs(rel) <= n, bias * np.float32(LOG2E), np.float32(NEG_INF)))
    return jnp.asarray(np.stack(tabs).astype(np.float32))


def _dil_body(*refs, seq_len):
    n_g = len(DIL_GROUPS)
    groups = [refs[7 * g:7 * g + 7] for g in range(n_g)]
    bias_refs = refs[7 * n_g:8 * n_g]
    o_ref, o_scr, l_scr = refs[8 * n_g:]
    i = pl.program_id(1)
    tq = o_ref.shape[2]
    halo = (DIL_K - DIL_Q) // 2

    for g, (_, d) in enumerate(DIL_GROUPS):
        q_ref, k_ref, kp_ref, kn_ref, v_ref, vp_ref, vn_ref = groups[g]
        n_piece = tq // (halo * d)
        n_sub = seq_len // d
        bias = bias_refs[g][0]

        def piece(cur, prv, nxt, e, j):
            if j < 0:
                return prv[0, 0, e * halo:(e + 1) * halo, :]
            if j >= n_piece:
                return nxt[0, 0, e * halo:(e + 1) * halo, :]
            r0 = (j * d + e) * halo
            return cur[0, 0, r0:r0 + halo, :]

        for e in range(d):
            for c in range(n_piece // 2):
                q = jnp.concatenate([piece(q_ref, None, None, e, j) for j in (2 * c, 2 * c + 1)], axis=0)
                k = jnp.concatenate([piece(k_ref, kp_ref, kn_ref, e, j) for j in range(2 * c - 1, 2 * c + 3)], axis=0)
                v = jnp.concatenate([piece(v_ref, vp_ref, vn_ref, e, j) for j in range(2 * c - 1, 2 * c + 3)], axis=0)
                s = _qk(q, k) + bias
                if c == 0 or c == n_piece // 2 - 1:
                    kpos = i * (tq // d) + (c * DIL_Q - halo) + lax.broadcasted_iota(jnp.int32, (1, DIL_K), 1)
                    s = jnp.where((kpos >= 0) & (kpos < n_sub), s, NEG_INF)
                o, lse = _masked_softmax_pv(s, v)
                rows = pl.ds(c * DIL_Q * d + e, DIL_Q, stride=d) if d > 1 else pl.ds(c * DIL_Q, DIL_Q)
                o_scr[g, rows, :] = o
                l_scr[g, rows, :] = jnp.broadcast_to(lse, (DIL_Q, HEAD_DIM))

    rows = 256

    def merge(r, carry):
        r0 = pl.multiple_of(r * rows, rows)
        ls = [l_scr[g, pl.ds(r0, rows), :] for g in range(n_g)]
        m = functools.reduce(jnp.maximum, ls)
        ws = [jnp.exp(l - m) for l in ls]
        num = sum(w * o_scr[g, pl.ds(r0, rows), :] for g, w in enumerate(ws))
        o_ref[0, 0, pl.ds(r0, rows), :] = (num / sum(ws)).astype(o_ref.dtype)
        return carry
    lax.fori_loop(0, tq // rows, merge, 0)


def _dil_attn(z, tab):
    b, _, l, _ = z.shape
    tq = min(DIL_TQ, l)
    halo = (DIL_K - DIL_Q) // 2
    assert l % tq == 0 and tq % 2048 == 0

    def cur(h0):
        return pl.BlockSpec((1, 1, tq, HEAD_DIM), lambda bi, i, h: (bi, h0 + h, i, 0))

    def prev(h0, rows):
        r = tq // rows
        return pl.BlockSpec((1, 1, rows, HEAD_DIM), lambda bi, i, h: (bi, h0 + h, jnp.maximum(i * r - 1, 0), 0))

    def nxt(h0, rows):
        r = tq // rows
        nblk = l // rows
        return pl.BlockSpec((1, 1, rows, HEAD_DIM),
                            lambda bi, i, h: (bi, h0 + h, jnp.minimum((i + 1) * r, nblk - 1), 0))

    in_specs = []
    for g, (_, d) in enumerate(DIL_GROUPS):
        hrows = halo * d
        in_specs += [cur(ZH_Q_DL[g]),
                     cur(ZH_K_DL[g]), prev(ZH_K_DL[g], hrows), nxt(ZH_K_DL[g], hrows),
                     cur(ZH_V_DL[g]), prev(ZH_V_DL[g], hrows), nxt(ZH_V_DL[g], hrows)]
    for g in range(len(DIL_GROUPS)):
        in_specs.append(pl.BlockSpec((1, DIL_Q, DIL_K), lambda bi, i, h, g=g: (g * N_HEADS_PER_DIL + h, 0, 0)))

    n_g = len(DIL_GROUPS)
    return pl.pallas_call(
        functools.partial(_dil_body, seq_len=l),
        grid=(b, l // tq, N_HEADS_PER_DIL),
        in_specs=in_specs,
        out_specs=pl.BlockSpec((1, 1, tq, HEAD_DIM), lambda bi, i, h: (bi, h, i, 0)),
        out_shape=jax.ShapeDtypeStruct((b, N_HEADS_PER_DIL, l, HEAD_DIM), BF16),
        scratch_shapes=[pltpu.VMEM((n_g, tq, HEAD_DIM), F32), pltpu.VMEM((n_g, tq, HEAD_DIM), F32)],
        compiler_params=_params("parallel", "parallel", "parallel"),
        name="dil_attn",
    )(*([z] * (7 * n_g)), *([tab] * n_g))


def _out_proj_body(ona_ref, odl_ref, gna_ref, gdl_ref, x_ref, gt_ref, wna_ref, wdl_ref, wo_ref, o_ref):
    def heads(ref):
        return jnp.concatenate([ref[0, h] for h in range(ref.shape[1])], axis=-1)

    y_na = jnp.dot(heads(ona_ref), wna_ref[...], preferred_element_type=F32)
    y_dl = jnp.dot(heads(odl_ref), wdl_ref[...], preferred_element_type=F32)
    merged = jax.nn.sigmoid(heads(gna_ref).astype(F32)) * y_na + jax.nn.sigmoid(heads(gdl_ref).astype(F32)) * y_dl
    mix = jnp.dot(merged.astype(BF16), wo_ref[...], preferred_element_type=F32)
    o_ref[0] = x_ref[0] + gt_ref[0] * mix


def _out_proj(o_na, o_dl, z, x, gt, w_na_out, w_dil_out, w_o):
    b, l, d = x.shape
    tm = min(OUT_TM, l)
    n_gate_heads = d // HEAD_DIM
    assert ZH_G_NA == 0 and ZH_G_DL == n_gate_heads

    def const(shape):
        return pl.BlockSpec(shape, lambda bi, i: (0, 0), pipeline_mode=pl.Buffered(1))

    return pl.pallas_call(
        _out_proj_body,
        grid=(b, l // tm),
        in_specs=[pl.BlockSpec((1, N_HEADS_NA, tm, HEAD_DIM), lambda bi, i: (bi, 0, i, 0)),
                  pl.BlockSpec((1, N_HEADS_PER_DIL, tm, HEAD_DIM), lambda bi, i: (bi, 0, i, 0)),
                  pl.BlockSpec((1, n_gate_heads, tm, HEAD_DIM), lambda bi, i: (bi, 0, i, 0)),
                  pl.BlockSpec((1, n_gate_heads, tm, HEAD_DIM), lambda bi, i: (bi, 1, i, 0)),
                  pl.BlockSpec((1, tm, d), lambda bi, i: (bi, i, 0)),
                  pl.BlockSpec((1, 1, d), lambda bi, i: (bi, 0, 0)),
                  const(w_na_out.shape), const(w_dil_out.shape), const(w_o.shape)],
        out_specs=pl.BlockSpec((1, tm, d), lambda bi, i: (bi, i, 0)),
        out_shape=jax.ShapeDtypeStruct((b, l, d), F32),
        compiler_params=_params("parallel", "parallel"),
        name="out_proj",
    )(o_na, o_dl, z, z, x, gt, w_na_out, w_dil_out, w_o)


def _ffn_body(x_ref, sh_ref, sc_ref, gt_ref, g_ref, gf_ref, w1_ref, w2_ref, o_ref, u_ref):
    f = pl.program_id(2)
    tm, d = u_ref.shape
    rows = NORM_ROWS

    @pl.when(f == 0)
    def _():
        def zero_acc(r0):
            o_ref[0, pl.ds(r0, rows), :] = jnp.zeros((rows, d), F32)
        _modulated_norm_rows(x_ref, u_ref, g_ref[...], sc_ref[0], sh_ref[0], on_chunk=zero_acc)

    h = jnp.dot(u_ref[...], w1_ref[...], preferred_element_type=F32)
    h = jnp.square(jnp.maximum(h, 0.0)).astype(BF16)
    for n0 in range(0, d, FFN_TN):
        o_ref[0, :, n0:n0 + FFN_TN] += jnp.dot(h, w2_ref[:, n0:n0 + FFN_TN], preferred_element_type=F32)

    @pl.when(f == pl.num_programs(2) - 1)
    def _():
        def chunk(r, carry):
            r0 = pl.multiple_of(r * rows, rows)
            x2 = x_ref[0, pl.ds(r0, rows), :] + gt_ref[0] * o_ref[0, pl.ds(r0, rows), :]
            o_ref[0, pl.ds(r0, rows), :] = x2
            rinv = lax.rsqrt(jnp.mean(x2 * x2, axis=-1, keepdims=True) + NORM_EPS)
            o_ref[0, pl.ds(r0, rows), :] = o_ref[0, pl.ds(r0, rows), :] * rinv * gf_ref[...]
            return carry
        lax.fori_loop(0, tm // rows, chunk, 0, unroll=2)


def _ffn(x, sh, sc, gt, g_ffn, g_final, w_ff1, w_ff2):
    b, l, d = x.shape
    tm = min(FFN_TM, l)
    tf = FFN_TF
    return pl.pallas_call(
        _ffn_body,
        grid=(b, l // tm, w_ff1.shape[1] // tf),
        in_specs=[pl.BlockSpec((1, tm, d), lambda bi, i, f: (bi, i, 0)),
                  pl.BlockSpec((1, 1, d), lambda bi, i, f: (bi, 0, 0)),
                  pl.BlockSpec((1, 1, d), lambda bi, i, f: (bi, 0, 0)),
                  pl.BlockSpec((1, 1, d), lambda bi, i, f: (bi, 0, 0)),
                  pl.BlockSpec((1, d), lambda bi, i, f: (0, 0)),
                  pl.BlockSpec((1, d), lambda bi, i, f: (0, 0)),
                  pl.BlockSpec((d, tf), lambda bi, i, f: (0, f)),
                  pl.BlockSpec((tf, d), lambda bi, i, f: (f, 0))],
        out_specs=pl.BlockSpec((1, tm, d), lambda bi, i, f: (bi, i, 0)),
        out_shape=jax.ShapeDtypeStruct((b, l, d), F32),
        scratch_shapes=[pltpu.VMEM((tm, d), BF16)],
        compiler_params=_params("parallel", "parallel", "arbitrary"),
        name="ffn",
    )(x, sh, sc, gt, g_ffn, g_final, w_ff1, w_ff2)


def _layer(x, mod, g_mix, w_in_r, na_tab, dil_tab, w_na_out, w_dil_out, w_o, g_ffn, w_ff1, w_ff2, g_final):
    d = x.shape[-1]
    sh1, sc1, gt1, sh2, sc2, gt2 = [mod[:, None, k * d:(k + 1) * d] for k in range(6)]
    z = _in_proj(x, sh1, sc1, g_mix, w_in_r)
    o_na = _na_attn(z, na_tab)
    o_dl = _dil_attn(z, dil_tab)
    x1 = _out_proj(o_na, o_dl, z, x, gt1, w_na_out, w_dil_out, w_o)
    return _ffn(x1, sh2, sc2, gt2, g_ffn, g_final, w_ff1, w_ff2)


def kernel(x_prompt, x_sample, c_prompt, c_sample, w_mod, b_mod, g_mix, w_in, rpb, w_na_out, w_dil_out, w_o,
           g_ffn, w_ff1, w_ff2, g_final):
    assert w_mod.shape[0] == 1, "single-layer trunk"
    d = x_prompt.shape[-1]
    nb_p, nb_s = c_prompt.shape[0], c_sample.shape[0]
    pad = (-(nb_p + nb_s)) % 8
    c_all = jnp.concatenate([c_prompt, c_sample, jnp.zeros((pad, d), F32)], axis=0)
    mod = _modulation(c_all, w_mod[0], b_mod[0])

    shared = (g_mix[0][None], _w_in_tiles(w_in[0], d), _na_bias_tables(rpb[0]), _dil_bias_tables(),
              w_na_out[0].astype(BF16), w_dil_out[0].astype(BF16), w_o[0].astype(BF16),
              g_ffn[0][None], w_ff1[0].astype(BF16), w_ff2[0].astype(BF16), g_final[None])
    y_prompt = _layer(x_prompt, mod[:nb_p], *shared)
    y_sample = _layer(x_sample, mod[nb_p:nb_p + nb_s], *shared)
    return (y_prompt, y_sample)
```

```python
import functools

import numpy as np
import jax
import jax.numpy as jnp
from jax import lax
from jax.experimental import pallas as pl
from jax.experimental.pallas import tpu as pltpu

F32 = jnp.float32
BF16 = jnp.bfloat16

HEAD_DIM = 128
N_HEADS_NA = 4
NA_ROWS = 8
NA_KW = 16
GRID_W = 64
DIL_GROUPS = ((128, 1), (512, 4), (2048, 16))
N_HEADS_PER_DIL = 4
N_HEADS_DIL = N_HEADS_PER_DIL * len(DIL_GROUPS)
ALIBI_MAX_BIAS = 8.0
NORM_EPS = 1e-6
NEG_INF = -1e30
ATTN_SCALE = HEAD_DIM ** -0.5
LOG2E = float(np.log2(np.e))
LN2 = float(np.log(2.0))
Q_PRESCALE = ATTN_SCALE * LOG2E

VMEM_LIMIT_BYTES = 60 * 1024 * 1024

ZH_G_NA, ZH_G_DL = 0, 16
ZH_Q_NA, ZH_K_NA, ZH_V_NA = 36, 40, 44
ZH_Q_DL = (52, 68, 32)
ZH_K_DL = (56, 72, 48)
ZH_V_DL = (60, 76, 64)
Z_HEADS = 80

IN_TM = 1024
IN_TN = 2048
IN_SUB = 512

NA_TQ = 1024
NA_GROUP = 4 * GRID_W
NA_KEYS = 12 * GRID_W
NA_HEADS_PER_STEP = 4
DIL_TQ = 2048
DIL_Q = 128
DIL_K = 256
RES_ROWS = (DIL_K - DIL_Q) // 2
DIL_HEADS_PER_STEP = 2
OUT_TM = 512
NORM_ROWS = 64
FFN_TM = 1024
FFN_TF = 512
FFN_TN = 512


def _params(*sem):
    return pltpu.CompilerParams(dimension_semantics=sem, vmem_limit_bytes=VMEM_LIMIT_BYTES)


def _w_in_column_order(d_model):
    d_na = N_HEADS_NA * HEAD_DIM
    d_dil = N_HEADS_DIL * HEAD_DIM
    d_grp = N_HEADS_PER_DIL * HEAD_DIM
    q_na, k_na, v_na = 0, d_na, 2 * d_na
    q_dl, k_dl, v_dl = 3 * d_na, 3 * d_na + d_dil, 3 * d_na + 2 * d_dil
    g_na = 3 * d_na + 3 * d_dil
    g_dl = g_na + d_model
    by_head = {ZH_G_NA: (g_na, d_model), ZH_G_DL: (g_dl, d_model),
               ZH_Q_NA: (q_na, d_na), ZH_K_NA: (k_na, d_na), ZH_V_NA: (v_na, d_na)}
    for g in range(len(DIL_GROUPS)):
        by_head[ZH_Q_DL[g]] = (q_dl + g * d_grp, d_grp)
        by_head[ZH_K_DL[g]] = (k_dl + g * d_grp, d_grp)
        by_head[ZH_V_DL[g]] = (v_dl + g * d_grp, d_grp)
    order, head = [], 0
    for h0 in sorted(by_head):
        assert h0 == head, "head-major layout must be contiguous"
        order.append(by_head[h0])
        head += by_head[h0][1] // HEAD_DIM
    assert head == Z_HEADS
    return order


def _in_tile_patterns():
    comp = {ZH_Q_NA: (1, True), ZH_K_NA: (1, False), ZH_V_NA: (1, False)}
    for g, (_, d) in enumerate(DIL_GROUPS):
        comp.update({ZH_Q_DL[g]: (d, True), ZH_K_DL[g]: (d, False), ZH_V_DL[g]: (d, False)})
    tile_heads, sub_heads = IN_TN // HEAD_DIM, IN_SUB // HEAD_DIM
    return tuple(tuple(comp.get(t * tile_heads + s * sub_heads, (1, False)) for s in range(IN_TN // IN_SUB))
                 for t in range(Z_HEADS // tile_heads))


def _w_in_tiles(w_in, d_model):
    tiles, parts, width = [], [], 0
    for c0, n in _w_in_column_order(d_model):
        while n > 0:
            take = min(n, IN_TN - width)
            parts.append(w_in[:, c0:c0 + take])
            c0, n, width = c0 + take, n - take, width + take
            if width == IN_TN:
                tiles.append(jnp.concatenate(parts, axis=1))
                parts, width = [], 0
    assert not parts
    return jnp.stack(tiles).astype(BF16)


def _mod_body(c_ref, w_ref, b_ref, o_ref):
    c = c_ref[...]
    s = (c * jax.nn.sigmoid(c)).astype(BF16)
    o_ref[...] = jnp.dot(s, w_ref[...].astype(BF16), preferred_element_type=F32) + b_ref[...]


def _modulation(c, w_mod, b_mod):
    rows, d = c.shape
    n = w_mod.shape[1]
    tn = 1024
    return pl.pallas_call(
        _mod_body,
        grid=(n // tn,),
        in_specs=[pl.BlockSpec((rows, d), lambda j: (0, 0)),
                  pl.BlockSpec((d, tn), lambda j: (0, j)),
                  pl.BlockSpec((1, tn), lambda j: (0, j))],
        out_specs=pl.BlockSpec((rows, tn), lambda j: (0, j)),
        out_shape=jax.ShapeDtypeStruct((rows, n), F32),
        compiler_params=_params("parallel"),
        name="modulation",
    )(c, w_mod, b_mod.reshape(1, n))


def _modulated_norm_rows(x_ref, u_ref, g, sc, sh, on_chunk=None):
    tm = u_ref.shape[0]
    gs = g * (1.0 + sc)

    def chunk(r, carry):
        r0 = pl.multiple_of(r * NORM_ROWS, NORM_ROWS)
        x = x_ref[0, pl.ds(r0, NORM_ROWS), :]
        rinv = lax.rsqrt(jnp.mean(x * x, axis=-1, keepdims=True) + NORM_EPS)
        u_ref[pl.ds(r0, NORM_ROWS), :] = (x_ref[0, pl.ds(r0, NORM_ROWS), :] * rinv * gs + sh).astype(BF16)
        if on_chunk is not None:
            on_chunk(r0)
        return carry
    lax.fori_loop(0, tm // NORM_ROWS, chunk, 0, unroll=2)


def _masked_softmax_pv(s2, v):
    m = jnp.max(s2, axis=-1, keepdims=True)
    p = jnp.exp2(s2 - m)
    l = jnp.sum(p, axis=-1, keepdims=True)
    o = jnp.dot(p.astype(BF16), v, preferred_element_type=F32) / l
    return o, m + jnp.log2(l)


def _qk(q, k):
    return lax.dot_general(q, k, (((1,), (1,)), ((), ())), preferred_element_type=F32)


def _in_proj_body(x_ref, sh_ref, sc_ref, g_ref, w_ref, z_ref, u_ref, p_ref, p2_ref):
    j = pl.program_id(2)
    tm = u_ref.shape[0]
    heads_per_dot = IN_SUB // HEAD_DIM

    @pl.when(j == 0)
    def _():
        _modulated_norm_rows(x_ref, u_ref, g_ref[...], sc_ref[0], sh_ref[0])

    def matmul(s):
        return jnp.dot(u_ref[...], w_ref[0, :, s * IN_SUB:(s + 1) * IN_SUB], preferred_element_type=F32)

    def write_out(s, acc, d, is_query):
        if is_query:
            acc = acc * Q_PRESCALE
        for hh in range(heads_per_dot):
            a = acc[:, hh * HEAD_DIM:(hh + 1) * HEAD_DIM]
            zh = s * heads_per_dot + hh
            if d == 1:
                z_ref[0, zh] = a.astype(BF16)
                continue
            p_ref[hh] = a
            src_ref = p_ref
            if d == 16:
                for e_lo in range(4):
                    p2_ref[hh, e_lo * (tm // 4):(e_lo + 1) * (tm // 4), :] = p_ref[hh, pl.ds(e_lo, tm // 4, stride=4), :]
                src_ref = p2_ref
            for blk in range(tm // (RES_ROWS * d)):
                for e in range(d):
                    r0 = (blk * d + e) * RES_ROWS
                    if d == 16:
                        src = pl.ds((e % 4) * (tm // 4) + blk * RES_ROWS * 4 + e // 4, RES_ROWS, stride=4)
                    else:
                        src = pl.ds(blk * RES_ROWS * d + e, RES_ROWS, stride=d)
                    z_ref[0, zh, r0:r0 + RES_ROWS, :] = src_ref[hh, src, :].astype(BF16)

    patterns = _in_tile_patterns()
    for pattern in sorted(set(patterns)):
        tiles = [t for t, k in enumerate(patterns) if k == pattern]
        cond = functools.reduce(jnp.logical_or, [j == t for t in tiles])

        @pl.when(cond)
        def _(pattern=pattern):
            pending = None
            for s, (d, is_query) in enumerate(pattern):
                acc = matmul(s)
                if pending is not None:
                    write_out(*pending)
                pending = (s, acc, d, is_query)
            write_out(*pending)


def _in_proj(x, sh, sc, g, w_in_r):
    b, l, d = x.shape
    tm = min(IN_TM, l)
    n_tiles = w_in_r.shape[0]
    assert l % tm == 0 and w_in_r.shape[1:] == (d, IN_TN) and n_tiles == len(_in_tile_patterns())
    assert all(tm % (RES_ROWS * dd) == 0 for pat in _in_tile_patterns() for dd, _ in pat)
    return pl.pallas_call(
        _in_proj_body,
        grid=(b, l // tm, n_tiles),
        in_specs=[pl.BlockSpec((1, tm, d), lambda bi, i, j: (bi, i, 0)),
                  pl.BlockSpec((1, 1, d), lambda bi, i, j: (bi, 0, 0)),
                  pl.BlockSpec((1, 1, d), lambda bi, i, j: (bi, 0, 0)),
                  pl.BlockSpec((1, d), lambda bi, i, j: (0, 0)),
                  pl.BlockSpec((1, d, IN_TN), lambda bi, i, j: (j, 0, 0))],
        out_specs=pl.BlockSpec((1, IN_TN // HEAD_DIM, tm, HEAD_DIM), lambda bi, i, j: (bi, j, i, 0)),
        out_shape=jax.ShapeDtypeStruct((b, Z_HEADS, l, HEAD_DIM), BF16),
        scratch_shapes=[pltpu.VMEM((tm, d), BF16)] + [pltpu.VMEM((IN_SUB // HEAD_DIM, tm, HEAD_DIM), F32)] * 2,
        compiler_params=_params("parallel", "parallel", "arbitrary"),
        name="in_proj",
    )(x, sh, sc, g, w_in_r)


def _na_bias_tables(rpb):
    rho = np.arange(4)[:, None, None, None]
    qc = np.arange(GRID_W)[None, :, None, None]
    f = np.arange(12)[None, None, :, None]
    kc = np.arange(GRID_W)[None, None, None, :]
    col_start = np.clip(qc - NA_KW // 2, 0, GRID_W - NA_KW)
    col_ok = (kc >= col_start) & (kc < col_start + NA_KW)
    row_ok = np.stack([
        (f >= rho) & (f < rho + NA_ROWS),
        (f >= 4) & (f < 4 + NA_ROWS) & (rho >= 0),
        (f >= 0) & (f < NA_ROWS) & (rho >= 0),
    ])
    ok = row_ok & col_ok[None]
    padded = jnp.pad(rpb.astype(F32), ((0, 0), (0, 0), (GRID_W, GRID_W)))
    c0 = GRID_W + NA_KW - 1
    col = jnp.stack([padded[:, :, c0 - q:c0 - q + GRID_W] for q in range(GRID_W)], axis=2)
    bias = jnp.stack([jnp.stack([col[:, fr - r + 3] for fr in range(12)], axis=2) for r in range(4)], axis=1)
    tab = jnp.where(ok[:, None], bias[None] * LOG2E, NEG_INF)
    return tab.reshape(3, rpb.shape[0], NA_GROUP, NA_KEYS)


def _na_body(q_ref, k_ref, kp_ref, kn_ref, v_ref, vp_ref, vn_ref, tab_ref, o_ref):
    i = pl.program_id(2)
    n_heads, tq = q_ref.shape[1], q_ref.shape[2]
    n_grp = tq // NA_GROUP

    def window(cur, prv, nxt, h, a):
        pieces = []
        for ga in (a - 1, a, a + 1):
            if ga < 0:
                pieces.append(prv[0, h])
            elif ga >= n_grp:
                pieces.append(nxt[0, h])
            else:
                pieces.append(cur[0, h, ga * NA_GROUP:(ga + 1) * NA_GROUP, :])
        return jnp.concatenate(pieces, axis=0)

    for h in range(n_heads):
        for a in range(n_grp):
            variant = 0
            if a == 0:
                variant = jnp.where(i == 0, 1, variant)
            if a == n_grp - 1:
                variant = jnp.where(i == pl.num_programs(2) - 1, 2, variant)
            q = q_ref[0, h, a * NA_GROUP:(a + 1) * NA_GROUP, :]
            s = _qk(q, window(k_ref, kp_ref, kn_ref, h, a)) + tab_ref[variant, h]
            o, _ = _masked_softmax_pv(s, window(v_ref, vp_ref, vn_ref, h, a))
            o_ref[0, h, a * NA_GROUP:(a + 1) * NA_GROUP, :] = o.astype(o_ref.dtype)


def _na_attn(z, tab):
    b, _, l, _ = z.shape
    tq = min(NA_TQ, l)
    halo = NA_GROUP
    n_halo_blocks = l // halo
    r = tq // halo
    hs = NA_HEADS_PER_STEP
    assert l % tq == 0 and l // NA_GROUP >= 3
    assert N_HEADS_NA % hs == 0 and all(h0 % hs == 0 for h0 in (ZH_Q_NA, ZH_K_NA, ZH_V_NA))

    def cur(h0):
        return pl.BlockSpec((1, hs, tq, HEAD_DIM), lambda bi, h, i: (bi, h0 // hs + h, i, 0))

    def prev(h0):
        return pl.BlockSpec((1, hs, halo, HEAD_DIM),
                            lambda bi, h, i: (bi, h0 // hs + h, jnp.maximum(i * r - 1, 0), 0))

    def nxt(h0):
        return pl.BlockSpec((1, hs, halo, HEAD_DIM),
                            lambda bi, h, i: (bi, h0 // hs + h, jnp.minimum((i + 1) * r, n_halo_blocks - 1), 0))

    return pl.pallas_call(
        _na_body,
        grid=(b, N_HEADS_NA // hs, l // tq),
        in_specs=[cur(ZH_Q_NA), cur(ZH_K_NA), prev(ZH_K_NA), nxt(ZH_K_NA),
                  cur(ZH_V_NA), prev(ZH_V_NA), nxt(ZH_V_NA),
                  pl.BlockSpec((3, hs, NA_GROUP, NA_KEYS), lambda bi, h, i: (0, h, 0, 0))],
        out_specs=pl.BlockSpec((1, hs, tq, HEAD_DIM), lambda bi, h, i: (bi, h, i, 0)),
        out_shape=jax.ShapeDtypeStruct((b, N_HEADS_NA, l, HEAD_DIM), BF16),
        compiler_params=_params("parallel", "parallel", "parallel"),
        name="na_attn",
    )(z, z, z, z, z, z, z, tab)


def _dil_bias_tables():
    h = np.arange(1, N_HEADS_DIL + 1, dtype=np.float32)
    slopes = np.power(np.float32(2.0), -ALIBI_MAX_BIAS * h / N_HEADS_DIL).astype(np.float32)
    rel = np.arange(DIL_K)[None, :] - (DIL_K - DIL_Q) // 2 - np.arange(DIL_Q)[:, None]
    tabs = []
    for g, (window, dilation) in enumerate(DIL_GROUPS):
        n = window // (2 * dilation)
        assert n == (DIL_K - DIL_Q) // 2
        dist = (np.abs(rel) * dilation).astype(np.float32)
        for jh in range(N_HEADS_PER_DIL):
            bias = -slopes[g * N_HEADS_PER_DIL + jh] * dist
            tabs.append(np.where(np.abs(rel) <= n, bias * np.float32(LOG2E), np.float32(NEG_INF)))
    return jnp.asarray(np.stack(tabs).astype(np.float32))


def _dil_body(*refs, seq_len):
    n_g = len(DIL_GROUPS)
    groups = [refs[7 * g:7 * g + 7] for g in range(n_g)]
    bias_refs = refs[7 * n_g:8 * n_g]
    o_ref, o_scr, l_scr = refs[8 * n_g:]
    i = pl.program_id(1)
    n_heads, tq = o_ref.shape[1], o_ref.shape[2]
    halo = (DIL_K - DIL_Q) // 2

    for h in range(n_heads):
        for g, (_, d) in enumerate(DIL_GROUPS):
            q_ref, k_ref, kp_ref, kn_ref, v_ref, vp_ref, vn_ref = groups[g]
            n_piece = tq // (halo * d)
            n_sub = seq_len // d
            bias = bias_refs[g][h]

            def piece(cur, prv, nxt, e, j):
                if j < 0:
                    return prv[0, h, e * halo:(e + 1) * halo, :]
                if j >= n_piece:
                    return nxt[0, h, e * halo:(e + 1) * halo, :]
                r0 = (j * d + e) * halo
                return cur[0, h, r0:r0 + halo, :]

            for e in range(d):
                for c in range(n_piece // 2):
                    q = jnp.concatenate([piece(q_ref, None, None, e, j) for j in (2 * c, 2 * c + 1)], axis=0)
                    k = jnp.concatenate([piece(k_ref, kp_ref, kn_ref, e, j) for j in range(2 * c - 1, 2 * c + 3)],
                                        axis=0)
                    v = jnp.concatenate([piece(v_ref, vp_ref, vn_ref, e, j) for j in range(2 * c - 1, 2 * c + 3)],
                                        axis=0)
                    s = _qk(q, k) + bias
                    if c == 0 or c == n_piece // 2 - 1:
                        kpos = i * (tq // d) + (c * DIL_Q - halo) + lax.broadcasted_iota(jnp.int32, (1, DIL_K), 1)
                        s = jnp.where((kpos >= 0) & (kpos < n_sub), s, NEG_INF)
                    o, lse2 = _masked_softmax_pv(s, v)
                    rows = pl.ds(c * DIL_Q * d + e, DIL_Q, stride=d) if d > 1 else pl.ds(c * DIL_Q, DIL_Q)
                    o_scr[h, g, rows, :] = o
                    l_scr[h, g, rows, :] = jnp.broadcast_to(lse2, (DIL_Q, HEAD_DIM))

    rows = 256

    def merge(r, carry):
        r0 = pl.multiple_of(r * rows, rows)
        for h in range(n_heads):
            ls = [l_scr[h, g, pl.ds(r0, rows), :] for g in range(n_g)]
            m = functools.reduce(jnp.maximum, ls)
            ws = [jnp.exp2(l - m) for l in ls]
            num = sum(w * o_scr[h, g, pl.ds(r0, rows), :] for g, w in enumerate(ws))
            o_ref[0, h, pl.ds(r0, rows), :] = (num / sum(ws)).astype(o_ref.dtype)
        return carry
    lax.fori_loop(0, tq // rows, merge, 0)


def _dil_attn(z, tab):
    b, _, l, _ = z.shape
    tq = min(DIL_TQ, l)
    halo = (DIL_K - DIL_Q) // 2
    hs = DIL_HEADS_PER_STEP
    assert l % tq == 0 and tq % 2048 == 0
    assert N_HEADS_PER_DIL % hs == 0 and all(h0 % hs == 0 for h0 in ZH_Q_DL + ZH_K_DL + ZH_V_DL)

    def cur(h0):
        return pl.BlockSpec((1, hs, tq, HEAD_DIM), lambda bi, i, h: (bi, h0 // hs + h, i, 0))

    def prev(h0, rows):
        r = tq // rows
        return pl.BlockSpec((1, hs, rows, HEAD_DIM),
                            lambda bi, i, h: (bi, h0 // hs + h, jnp.maximum(i * r - 1, 0), 0))

    def nxt(h0, rows):
        r = tq // rows
        nblk = l // rows
        return pl.BlockSpec((1, hs, rows, HEAD_DIM),
                            lambda bi, i, h: (bi, h0 // hs + h, jnp.minimum((i + 1) * r, nblk - 1), 0))

    in_specs = []
    for g, (_, d) in enumerate(DIL_GROUPS):
        hrows = halo * d
        in_specs += [cur(ZH_Q_DL[g]),
                     cur(ZH_K_DL[g]), prev(ZH_K_DL[g], hrows), nxt(ZH_K_DL[g], hrows),
                     cur(ZH_V_DL[g]), prev(ZH_V_DL[g], hrows), nxt(ZH_V_DL[g], hrows)]
    for g in range(len(DIL_GROUPS)):
        in_specs.append(pl.BlockSpec((hs, DIL_Q, DIL_K),
                                     lambda bi, i, h, g=g: (g * (N_HEADS_PER_DIL // hs) + h, 0, 0)))

    n_g = len(DIL_GROUPS)
    return pl.pallas_call(
        functools.partial(_dil_body, seq_len=l),
        grid=(b, l // tq, N_HEADS_PER_DIL // hs),
        in_specs=in_specs,
        out_specs=pl.BlockSpec((1, hs, tq, HEAD_DIM), lambda bi, i, h: (bi, h, i, 0)),
        out_shape=jax.ShapeDtypeStruct((b, N_HEADS_PER_DIL, l, HEAD_DIM), BF16),
        scratch_shapes=[pltpu.VMEM((hs, n_g, tq, HEAD_DIM), F32)] * 2,
        compiler_params=_params("parallel", "parallel", "parallel"),
        name="dil_attn",
    )(*([z] * (7 * n_g)), *([tab] * n_g))


def _out_proj_body(ona_ref, odl_ref, gna_ref, gdl_ref, x_ref, gt_ref, wna_ref, wdl_ref, wo_ref, o_ref):
    def heads(ref):
        return jnp.concatenate([ref[0, h] for h in range(ref.shape[1])], axis=-1)

    y_na = jnp.dot(heads(ona_ref), wna_ref[...], preferred_element_type=F32)
    y_dl = jnp.dot(heads(odl_ref), wdl_ref[...], preferred_element_type=F32)
    merged = jax.nn.sigmoid(heads(gna_ref).astype(F32)) * y_na + jax.nn.sigmoid(heads(gdl_ref).astype(F32)) * y_dl
    mix = jnp.dot(merged.astype(BF16), wo_ref[...], preferred_element_type=F32)
    o_ref[0] = x_ref[0] + gt_ref[0] * mix


def _out_proj(o_na, o_dl, z, x, gt, w_na_out, w_dil_out, w_o):
    b, l, d = x.shape
    tm = min(OUT_TM, l)
    n_gate_heads = d // HEAD_DIM
    assert ZH_G_NA == 0 and ZH_G_DL == n_gate_heads

    def const(shape):
        return pl.BlockSpec(shape, lambda bi, i: (0, 0), pipeline_mode=pl.Buffered(1))

    return pl.pallas_call(
        _out_proj_body,
        grid=(b, l // tm),
        in_specs=[pl.BlockSpec((1, N_HEADS_NA, tm, HEAD_DIM), lambda bi, i: (bi, 0, i, 0)),
                  pl.BlockSpec((1, N_HEADS_PER_DIL, tm, HEAD_DIM), lambda bi, i: (bi, 0, i, 0)),
                  pl.BlockSpec((1, n_gate_heads, tm, HEAD_DIM), lambda bi, i: (bi, 0, i, 0)),
                  pl.BlockSpec((1, n_gate_heads, tm, HEAD_DIM), lambda bi, i: (bi, 1, i, 0)),
                  pl.BlockSpec((1, tm, d), lambda bi, i: (bi, i, 0)),
                  pl.BlockSpec((1, 1, d), lambda bi, i: (bi, 0, 0)),
                  const(w_na_out.shape), const(w_dil_out.shape), const(w_o.shape)],
        out_specs=pl.BlockSpec((1, tm, d), lambda bi, i: (bi, i, 0)),
        out_shape=jax.ShapeDtypeStruct((b, l, d), F32),
        compiler_params=_params("parallel", "parallel"),
        name="out_proj",
    )(o_na, o_dl, z, z, x, gt, w_na_out, w_dil_out, w_o)


def _ffn_body(x_ref, sh_ref, sc_ref, gt_ref, g_ref, gf_ref, w1_ref, w2_ref, o_ref, u_ref):
    f = pl.program_id(2)
    tm, d = u_ref.shape
    rows = NORM_ROWS

    @pl.when(f == 0)
    def _():
        def zero_acc(r0):
            o_ref[0, pl.ds(r0, rows), :] = jnp.zeros((rows, d), F32)
        _modulated_norm_rows(x_ref, u_ref, g_ref[...], sc_ref[0], sh_ref[0], on_chunk=zero_acc)

    h = jnp.dot(u_ref[...], w1_ref[...], preferred_element_type=F32)
    h = jnp.square(jnp.maximum(h, 0.0)).astype(BF16)
    for n0 in range(0, d, FFN_TN):
        o_ref[0, :, n0:n0 + FFN_TN] += jnp.dot(h, w2_ref[:, n0:n0 + FFN_TN], preferred_element_type=F32)

    @pl.when(f == pl.num_programs(2) - 1)
    def _():
        def chunk(r, carry):
            r0 = pl.multiple_of(r * rows, rows)
            x2 = x_ref[0, pl.ds(r0, rows), :] + gt_ref[0] * o_ref[0, pl.ds(r0, rows), :]
            o_ref[0, pl.ds(r0, rows), :] = x2
            rinv = lax.rsqrt(jnp.mean(x2 * x2, axis=-1, keepdims=True) + NORM_EPS)
            o_ref[0, pl.ds(r0, rows), :] = o_ref[0, pl.ds(r0, rows), :] * rinv * gf_ref[...]
            return carry
        lax.fori_loop(0, tm // rows, chunk, 0, unroll=2)


def _ffn(x, sh, sc, gt, g_ffn, g_final, w_ff1, w_ff2):
    b, l, d = x.shape
    tm = min(FFN_TM, l)
    tf = FFN_TF
    return pl.pallas_call(
        _ffn_body,
        grid=(b, l // tm, w_ff1.shape[1] // tf),
        in_specs=[pl.BlockSpec((1, tm, d), lambda bi, i, f: (bi, i, 0)),
                  pl.BlockSpec((1, 1, d), lambda bi, i, f: (bi, 0, 0)),
                  pl.BlockSpec((1, 1, d), lambda bi, i, f: (bi, 0, 0)),
                  pl.BlockSpec((1, 1, d), lambda bi, i, f: (bi, 0, 0)),
                  pl.BlockSpec((1, d), lambda bi, i, f: (0, 0)),
                  pl.BlockSpec((1, d), lambda bi, i, f: (0, 0)),
                  pl.BlockSpec((d, tf), lambda bi, i, f: (0, f)),
                  pl.BlockSpec((tf, d), lambda bi, i, f: (f, 0))],
        out_specs=pl.BlockSpec((1, tm, d), lambda bi, i, f: (bi, i, 0)),
        out_shape=jax.ShapeDtypeStruct((b, l, d), F32),
        scratch_shapes=[pltpu.VMEM((tm, d), BF16)],
        compiler_params=_params("parallel", "parallel", "arbitrary"),
        name="ffn",
    )(x, sh, sc, gt, g_ffn, g_final, w_ff1, w_ff2)


def _layer(x, mod, g_mix, w_in_r, na_tab, dil_tab, w_na_out, w_dil_out, w_o, g_ffn, w_ff1, w_ff2, g_final):
    d = x.shape[-1]
    sh1, sc1, gt1, sh2, sc2, gt2 = [mod[:, None, k * d:(k + 1) * d] for k in range(6)]
    z = _in_proj(x, sh1, sc1, g_mix, w_in_r)
    o_na = _na_attn(z, na_tab)
    o_dl = _dil_attn(z, dil_tab)
    x1 = _out_proj(o_na, o_dl, z, x, gt1, w_na_out, w_dil_out, w_o)
    return _ffn(x1, sh2, sc2, gt2, g_ffn, g_final, w_ff1, w_ff2)


def kernel(x_prompt, x_sample, c_prompt, c_sample, w_mod, b_mod, g_mix, w_in, rpb, w_na_out, w_dil_out, w_o,
           g_ffn, w_ff1, w_ff2, g_final):
    assert w_mod.shape[0] == 1, "single-layer trunk"
    d = x_prompt.shape[-1]
    nb_p, nb_s = c_prompt.shape[0], c_sample.shape[0]
    pad = (-(nb_p + nb_s)) % 8
    c_all = jnp.concatenate([c_prompt, c_sample, jnp.zeros((pad, d), F32)], axis=0)
    mod = _modulation(c_all, w_mod[0], b_mod[0])

    shared = (g_mix[0][None], _w_in_tiles(w_in[0], d), _na_bias_tables(rpb[0]), _dil_bias_tables(),
              w_na_out[0].astype(BF16), w_dil_out[0].astype(BF16), w_o[0].astype(BF16),
              g_ffn[0][None], w_ff1[0].astype(BF16), w_ff2[0].astype(BF16), g_final[None])
    y_prompt = _layer(x_prompt, mod[:nb_p], *shared)
    y_sample = _layer(x_sample, mod[nb_p:nb_p + nb_s], *shared)
    return (y_prompt, y_sample)
```

```python
import functools

import numpy as np
import jax
import jax.numpy as jnp
from jax import lax
from jax.experimental import pallas as pl
from jax.experimental.pallas import tpu as pltpu

F32 = jnp.float32
BF16 = jnp.bfloat16

HEAD_DIM = 128
N_HEADS_NA = 4
NA_ROWS = 8
NA_KW = 16
GRID_W = 64
DIL_GROUPS = ((128, 1), (512, 4), (2048, 16))
N_HEADS_PER_DIL = 4
N_HEADS_DIL = N_HEADS_PER_DIL * len(DIL_GROUPS)
ALIBI_MAX_BIAS = 8.0
NORM_EPS = 1e-6
NEG_INF = -1e30
ATTN_SCALE = HEAD_DIM ** -0.5
LOG2E = float(np.log2(np.e))
LN2 = float(np.log(2.0))
Q_PRESCALE = ATTN_SCALE * LOG2E

VMEM_LIMIT_BYTES = 60 * 1024 * 1024

ZH_G_NA, ZH_G_DL = 0, 16
ZH_Q_NA, ZH_K_NA, ZH_V_NA = 36, 40, 44
ZH_Q_DL = (52, 68, 32)
ZH_K_DL = (56, 72, 48)
ZH_V_DL = (60, 76, 64)
Z_HEADS = 80

IN_TM = 1024
IN_TN = 2048
IN_SUB = 512

NA_TQ = 1024
NA_GROUP = 4 * GRID_W
NA_KEYS = 12 * GRID_W
NA_HEADS_PER_STEP = 4
DIL_TQ = 2048
DIL_Q = 128
DIL_K = 256
RES_ROWS = (DIL_K - DIL_Q) // 2
DIL_HEADS_PER_STEP = 2
OUT_TM = 512
NORM_ROWS = 64
FFN_TM = 1024
FFN_TF = 1024
FFN_TN = 512


def _params(*sem):
    return pltpu.CompilerParams(dimension_semantics=sem, vmem_limit_bytes=VMEM_LIMIT_BYTES)


def _w_in_column_order(d_model):
    d_na = N_HEADS_NA * HEAD_DIM
    d_dil = N_HEADS_DIL * HEAD_DIM
    d_grp = N_HEADS_PER_DIL * HEAD_DIM
    q_na, k_na, v_na = 0, d_na, 2 * d_na
    q_dl, k_dl, v_dl = 3 * d_na, 3 * d_na + d_dil, 3 * d_na + 2 * d_dil
    g_na = 3 * d_na + 3 * d_dil
    g_dl = g_na + d_model
    by_head = {ZH_G_NA: (g_na, d_model), ZH_G_DL: (g_dl, d_model),
               ZH_Q_NA: (q_na, d_na), ZH_K_NA: (k_na, d_na), ZH_V_NA: (v_na, d_na)}
    for g in range(len(DIL_GROUPS)):
        by_head[ZH_Q_DL[g]] = (q_dl + g * d_grp, d_grp)
        by_head[ZH_K_DL[g]] = (k_dl + g * d_grp, d_grp)
        by_head[ZH_V_DL[g]] = (v_dl + g * d_grp, d_grp)
    order, head = [], 0
    for h0 in sorted(by_head):
        assert h0 == head, "head-major layout must be contiguous"
        order.append(by_head[h0])
        head += by_head[h0][1] // HEAD_DIM
    assert head == Z_HEADS
    return order


def _in_tile_patterns():
    comp = {ZH_Q_NA: (1, True), ZH_K_NA: (1, False), ZH_V_NA: (1, False)}
    for g, (_, d) in enumerate(DIL_GROUPS):
        comp.update({ZH_Q_DL[g]: (d, True), ZH_K_DL[g]: (d, False), ZH_V_DL[g]: (d, False)})
    tile_heads, sub_heads = IN_TN // HEAD_DIM, IN_SUB // HEAD_DIM
    return tuple(tuple(comp.get(t * tile_heads + s * sub_heads, (1, False)) for s in range(IN_TN // IN_SUB))
                 for t in range(Z_HEADS // tile_heads))


def _w_in_tiles(w_in, d_model):
    tiles, parts, width = [], [], 0
    for c0, n in _w_in_column_order(d_model):
        while n > 0:
            take = min(n, IN_TN - width)
            parts.append(w_in[:, c0:c0 + take])
            c0, n, width = c0 + take, n - take, width + take
            if width == IN_TN:
                tiles.append(jnp.concatenate(parts, axis=1))
                parts, width = [], 0
    assert not parts
    return jnp.stack(tiles).astype(BF16)


def _mod_body(c_ref, w_ref, b_ref, o_ref):
    c = c_ref[...]
    s = (c * jax.nn.sigmoid(c)).astype(BF16)
    o_ref[...] = jnp.dot(s, w_ref[...].astype(BF16), preferred_element_type=F32) + b_ref[...]


def _modulation(c, w_mod, b_mod):
    rows, d = c.shape
    n = w_mod.shape[1]
    tn = 1024
    return pl.pallas_call(
        _mod_body,
        grid=(n // tn,),
        in_specs=[pl.BlockSpec((rows, d), lambda j: (0, 0)),
                  pl.BlockSpec((d, tn), lambda j: (0, j)),
                  pl.BlockSpec((1, tn), lambda j: (0, j))],
        out_specs=pl.BlockSpec((rows, tn), lambda j: (0, j)),
        out_shape=jax.ShapeDtypeStruct((rows, n), F32),
        compiler_params=_params("parallel"),
        name="modulation",
    )(c, w_mod, b_mod.reshape(1, n))


def _modulated_norm_rows(x_ref, u_ref, g, sc, sh, on_chunk=None):
    tm = u_ref.shape[0]
    gs = g * (1.0 + sc)

    def chunk(r, carry):
        r0 = pl.multiple_of(r * NORM_ROWS, NORM_ROWS)
        x = x_ref[0, pl.ds(r0, NORM_ROWS), :]
        rinv = lax.rsqrt(jnp.mean(x * x, axis=-1, keepdims=True) + NORM_EPS)
        u_ref[pl.ds(r0, NORM_ROWS), :] = (x_ref[0, pl.ds(r0, NORM_ROWS), :] * rinv * gs + sh).astype(BF16)
        if on_chunk is not None:
            on_chunk(r0)
        return carry
    lax.fori_loop(0, tm // NORM_ROWS, chunk, 0, unroll=2)


def _masked_softmax_pv(s2, v):
    m = jnp.max(s2, axis=-1, keepdims=True)
    p = jnp.exp2(s2 - m)
    l = jnp.sum(p, axis=-1, keepdims=True)
    o = jnp.dot(p.astype(BF16), v, preferred_element_type=F32) / l
    return o, m + jnp.log2(l)


def _qk(q, k):
    return lax.dot_general(q, k, (((1,), (1,)), ((), ())), preferred_element_type=F32)


def _in_proj_body(x_ref, sh_ref, sc_ref, g_ref, w_ref, z_ref, u_ref, p_ref, p2_ref):
    j = pl.program_id(2)
    tm = u_ref.shape[0]
    heads_per_dot = IN_SUB // HEAD_DIM

    @pl.when(j == 0)
    def _():
        _modulated_norm_rows(x_ref, u_ref, g_ref[...], sc_ref[0], sh_ref[0])

    def matmul(s):
        return jnp.dot(u_ref[...], w_ref[0, :, s * IN_SUB:(s + 1) * IN_SUB], preferred_element_type=F32)

    def write_out(s, acc, d, is_query):
        if is_query:
            acc = acc * Q_PRESCALE
        for hh in range(heads_per_dot):
            a = acc[:, hh * HEAD_DIM:(hh + 1) * HEAD_DIM]
            zh = s * heads_per_dot + hh
            if d == 1:
                z_ref[0, zh] = a.astype(BF16)
                continue
            p_ref[hh] = a
            src_ref = p_ref
            if d == 16:
                for e_lo in range(4):
                    p2_ref[hh, e_lo * (tm // 4):(e_lo + 1) * (tm // 4), :] = p_ref[hh, pl.ds(e_lo, tm // 4, stride=4), :]
                src_ref = p2_ref
            for blk in range(tm // (RES_ROWS * d)):
                for e in range(d):
                    r0 = (blk * d + e) * RES_ROWS
                    if d == 16:
                        src = pl.ds((e % 4) * (tm // 4) + blk * RES_ROWS * 4 + e // 4, RES_ROWS, stride=4)
                    else:
                        src = pl.ds(blk * RES_ROWS * d + e, RES_ROWS, stride=d)
                    z_ref[0, zh, r0:r0 + RES_ROWS, :] = src_ref[hh, src, :].astype(BF16)

    patterns = _in_tile_patterns()
    for pattern in sorted(set(patterns)):
        tiles = [t for t, k in enumerate(patterns) if k == pattern]
        cond = functools.reduce(jnp.logical_or, [j == t for t in tiles])

        @pl.when(cond)
        def _(pattern=pattern):
            pending = None
            for s, (d, is_query) in enumerate(pattern):
                acc = matmul(s)
                if pending is not None:
                    write_out(*pending)
                pending = (s, acc, d, is_query)
            write_out(*pending)


def _in_proj(x, sh, sc, g, w_in_r):
    b, l, d = x.shape
    tm = min(IN_TM, l)
    n_tiles = w_in_r.shape[0]
    assert l % tm == 0 and w_in_r.shape[1:] == (d, IN_TN) and n_tiles == len(_in_tile_patterns())
    assert all(tm % (RES_ROWS * dd) == 0 for pat in _in_tile_patterns() for dd, _ in pat)
    return pl.pallas_call(
        _in_proj_body,
        grid=(b, l // tm, n_tiles),
        in_specs=[pl.BlockSpec((1, tm, d), lambda bi, i, j: (bi, i, 0)),
                  pl.BlockSpec((1, 1, d), lambda bi, i, j: (bi, 0, 0)),
                  pl.BlockSpec((1, 1, d), lambda bi, i, j: (bi, 0, 0)),
                  pl.BlockSpec((1, d), lambda bi, i, j: (0, 0)),
                  pl.BlockSpec((1, d, IN_TN), lambda bi, i, j: (j, 0, 0))],
        out_specs=pl.BlockSpec((1, IN_TN // HEAD_DIM, tm, HEAD_DIM), lambda bi, i, j: (bi, j, i, 0)),
        out_shape=jax.ShapeDtypeStruct((b, Z_HEADS, l, HEAD_DIM), BF16),
        scratch_shapes=[pltpu.VMEM((tm, d), BF16)] + [pltpu.VMEM((IN_SUB // HEAD_DIM, tm, HEAD_DIM), F32)] * 2,
        compiler_params=_params("parallel", "parallel", "arbitrary"),
        name="in_proj",
    )(x, sh, sc, g, w_in_r)


def _na_bias_tables(rpb):
    rho = np.arange(4)[:, None, None, None]
    qc = np.arange(GRID_W)[None, :, None, None]
    f = np.arange(12)[None, None, :, None]
    kc = np.arange(GRID_W)[None, None, None, :]
    col_start = np.clip(qc - NA_KW // 2, 0, GRID_W - NA_KW)
    col_ok = (kc >= col_start) & (kc < col_start + NA_KW)
    row_ok = np.stack([
        (f >= rho) & (f < rho + NA_ROWS),
        (f >= 4) & (f < 4 + NA_ROWS) & (rho >= 0),
        (f >= 0) & (f < NA_ROWS) & (rho >= 0),
    ])
    ok = row_ok & col_ok[None]
    padded = jnp.pad(rpb.astype(F32), ((0, 0), (0, 0), (GRID_W, GRID_W)))
    c0 = GRID_W + NA_KW - 1
    col = jnp.stack([padded[:, :, c0 - q:c0 - q + GRID_W] for q in range(GRID_W)], axis=2)
    bias = jnp.stack([jnp.stack([col[:, fr - r + 3] for fr in range(12)], axis=2) for r in range(4)], axis=1)
    tab = jnp.where(ok[:, None], bias[None] * LOG2E, NEG_INF)
    return tab.reshape(3, rpb.shape[0], NA_GROUP, NA_KEYS)


def _na_body(q_ref, k_ref, kp_ref, kn_ref, v_ref, vp_ref, vn_ref, tab_ref, o_ref):
    i = pl.program_id(2)
    n_heads, tq = q_ref.shape[1], q_ref.shape[2]
    n_grp = tq // NA_GROUP

    def window(cur, prv, nxt, h, a):
        pieces = []
        for ga in (a - 1, a, a + 1):
            if ga < 0:
                pieces.append(prv[0, h])
            elif ga >= n_grp:
                pieces.append(nxt[0, h])
            else:
                pieces.append(cur[0, h, ga * NA_GROUP:(ga + 1) * NA_GROUP, :])
        return jnp.concatenate(pieces, axis=0)

    for h in range(n_heads):
        for a in range(n_grp):
            variant = 0
            if a == 0:
                variant = jnp.where(i == 0, 1, variant)
            if a == n_grp - 1:
                variant = jnp.where(i == pl.num_programs(2) - 1, 2, variant)
            q = q_ref[0, h, a * NA_GROUP:(a + 1) * NA_GROUP, :]
            s = _qk(q, window(k_ref, kp_ref, kn_ref, h, a)) + tab_ref[variant, h]
            o, _ = _masked_softmax_pv(s, window(v_ref, vp_ref, vn_ref, h, a))
            o_ref[0, h, a * NA_GROUP:(a + 1) * NA_GROUP, :] = o.astype(o_ref.dtype)


def _na_attn(z, tab):
    b, _, l, _ = z.shape
    tq = min(NA_TQ, l)
    halo = NA_GROUP
    n_halo_blocks = l // halo
    r = tq // halo
    hs = NA_HEADS_PER_STEP
    assert l % tq == 0 and l // NA_GROUP >= 3
    assert N_HEADS_NA % hs == 0 and all(h0 % hs == 0 for h0 in (ZH_Q_NA, ZH_K_NA, ZH_V_NA))

    def cur(h0):
        return pl.BlockSpec((1, hs, tq, HEAD_DIM), lambda bi, h, i: (bi, h0 // hs + h, i, 0))

    def prev(h0):
        return pl.BlockSpec((1, hs, halo, HEAD_DIM),
                            lambda bi, h, i: (bi, h0 // hs + h, jnp.maximum(i * r - 1, 0), 0))

    def nxt(h0):
        return pl.BlockSpec((1, hs, halo, HEAD_DIM),
                            lambda bi, h, i: (bi, h0 // hs + h, jnp.minimum((i + 1) * r, n_halo_blocks - 1), 0))

    return pl.pallas_call(
        _na_body,
        grid=(b, N_HEADS_NA // hs, l // tq),
        in_specs=[cur(ZH_Q_NA), cur(ZH_K_NA), prev(ZH_K_NA), nxt(ZH_K_NA),
                  cur(ZH_V_NA), prev(ZH_V_NA), nxt(ZH_V_NA),
                  pl.BlockSpec((3, hs, NA_GROUP, NA_KEYS), lambda bi, h, i: (0, h, 0, 0))],
        out_specs=pl.BlockSpec((1, hs, tq, HEAD_DIM), lambda bi, h, i: (bi, h, i, 0)),
        out_shape=jax.ShapeDtypeStruct((b, N_HEADS_NA, l, HEAD_DIM), BF16),
        compiler_params=_params("parallel", "parallel", "parallel"),
        name="na_attn",
    )(z, z, z, z, z, z, z, tab)


def _dil_bias_tables():
    h = np.arange(1, N_HEADS_DIL + 1, dtype=np.float32)
    slopes = np.power(np.float32(2.0), -ALIBI_MAX_BIAS * h / N_HEADS_DIL).astype(np.float32)
    rel = np.arange(DIL_K)[None, :] - (DIL_K - DIL_Q) // 2 - np.arange(DIL_Q)[:, None]
    tabs = []
    for g, (window, dilation) in enumerate(DIL_GROUPS):
        n = window // (2 * dilation)
        assert n == (DIL_K - DIL_Q) // 2
        dist = (np.abs(rel) * dilation).astype(np.float32)
        for jh in range(N_HEADS_PER_DIL):
            bias = -slopes[g * N_HEADS_PER_DIL + jh] * dist
            tabs.append(np.where(np.abs(rel) <= n, bias * np.float32(LOG2E), np.float32(NEG_INF)))
    return jnp.asarray(np.stack(tabs).astype(np.float32))


def _dil_body(*refs, seq_len):
    n_g = len(DIL_GROUPS)
    groups = [refs[7 * g:7 * g + 7] for g in range(n_g)]
    bias_refs = refs[7 * n_g:8 * n_g]
    o_ref, o_scr, l_scr = refs[8 * n_g:]
    i = pl.program_id(1)
    n_heads, tq = o_ref.shape[1], o_ref.shape[2]
    halo = (DIL_K - DIL_Q) // 2

    for h in range(n_heads):
        for g, (_, d) in enumerate(DIL_GROUPS):
            q_ref, k_ref, kp_ref, kn_ref, v_ref, vp_ref, vn_ref = groups[g]
            n_piece = tq // (halo * d)
            n_sub = seq_len // d
            bias = bias_refs[g][h]

            def piece(cur, prv, nxt, e, j):
                if j < 0:
                    return prv[0, h, e * halo:(e + 1) * halo, :]
                if j >= n_piece:
                    return nxt[0, h, e * halo:(e + 1) * halo, :]
                r0 = (j * d + e) * halo
                return cur[0, h, r0:r0 + halo, :]

            for e in range(d):
                for c in range(n_piece // 2):
                    q = jnp.concatenate([piece(q_ref, None, None, e, j) for j in (2 * c, 2 * c + 1)], axis=0)
                    k = jnp.concatenate([piece(k_ref, kp_ref, kn_ref, e, j) for j in range(2 * c - 1, 2 * c + 3)],
                                        axis=0)
                    v = jnp.concatenate([piece(v_ref, vp_ref, vn_ref, e, j) for j in range(2 * c - 1, 2 * c + 3)],
                                        axis=0)
                    s = _qk(q, k) + bias
                    if c == 0 or c == n_piece // 2 - 1:
                        kpos = i * (tq // d) + (c * DIL_Q - halo) + lax.broadcasted_iota(jnp.int32, (1, DIL_K), 1)
                        s = jnp.where((kpos >= 0) & (kpos < n_sub), s, NEG_INF)
                    o, lse2 = _masked_softmax_pv(s, v)
                    rows = pl.ds(c * DIL_Q * d + e, DIL_Q, stride=d) if d > 1 else pl.ds(c * DIL_Q, DIL_Q)
                    o_scr[h, g, rows, :] = o
                    l_scr[h, g, rows, :] = jnp.broadcast_to(lse2, (DIL_Q, HEAD_DIM))

    rows = 256

    def merge(r, carry):
        r0 = pl.multiple_of(r * rows, rows)
        for h in range(n_heads):
            ls = [l_scr[h, g, pl.ds(r0, rows), :] for g in range(n_g)]
            m = functools.reduce(jnp.maximum, ls)
            ws = [jnp.exp2(l - m) for l in ls]
            num = sum(w * o_scr[h, g, pl.ds(r0, rows), :] for g, w in enumerate(ws))
            o_ref[0, h, pl.ds(r0, rows), :] = (num / sum(ws)).astype(o_ref.dtype)
        return carry
    lax.fori_loop(0, tq // rows, merge, 0)


def _dil_attn(z, tab):
    b, _, l, _ = z.shape
    tq = min(DIL_TQ, l)
    halo = (DIL_K - DIL_Q) // 2
    hs = DIL_HEADS_PER_STEP
    assert l % tq == 0 and tq % 2048 == 0
    assert N_HEADS_PER_DIL % hs == 0 and all(h0 % hs == 0 for h0 in ZH_Q_DL + ZH_K_DL + ZH_V_DL)

    def cur(h0):
        return pl.BlockSpec((1, hs, tq, HEAD_DIM), lambda bi, i, h: (bi, h0 // hs + h, i, 0))

    def prev(h0, rows):
        r = tq // rows
        return pl.BlockSpec((1, hs, rows, HEAD_DIM),
                            lambda bi, i, h: (bi, h0 // hs + h, jnp.maximum(i * r - 1, 0), 0))

    def nxt(h0, rows):
        r = tq // rows
        nblk = l // rows
        return pl.BlockSpec((1, hs, rows, HEAD_DIM),
                            lambda bi, i, h: (bi, h0 // hs + h, jnp.minimum((i + 1) * r, nblk - 1), 0))

    in_specs = []
    for g, (_, d) in enumerate(DIL_GROUPS):
        hrows = halo * d
        in_specs += [cur(ZH_Q_DL[g]),
                     cur(ZH_K_DL[g]), prev(ZH_K_DL[g], hrows), nxt(ZH_K_DL[g], hrows),
                     cur(ZH_V_DL[g]), prev(ZH_V_DL[g], hrows), nxt(ZH_V_DL[g], hrows)]
    for g in range(len(DIL_GROUPS)):
        in_specs.append(pl.BlockSpec((hs, DIL_Q, DIL_K),
                                     lambda bi, i, h, g=g: (g * (N_HEADS_PER_DIL // hs) + h, 0, 0)))

    n_g = len(DIL_GROUPS)
    return pl.pallas_call(
        functools.partial(_dil_body, seq_len=l),
        grid=(b, l // tq, N_HEADS_PER_DIL // hs),
        in_specs=in_specs,
        out_specs=pl.BlockSpec((1, hs, tq, HEAD_DIM), lambda bi, i, h: (bi, h, i, 0)),
        out_shape=jax.ShapeDtypeStruct((b, N_HEADS_PER_DIL, l, HEAD_DIM), BF16),
        scratch_shapes=[pltpu.VMEM((hs, n_g, tq, HEAD_DIM), F32)] * 2,
        compiler_params=_params("parallel", "parallel", "parallel"),
        name="dil_attn",
    )(*([z] * (7 * n_g)), *([tab] * n_g))


def _out_proj_body(ona_ref, odl_ref, gna_ref, gdl_ref, x_ref, gt_ref, wna_ref, wdl_ref, wo_ref, o_ref):
    def heads(ref):
        return jnp.concatenate([ref[0, h] for h in range(ref.shape[1])], axis=-1)

    y_na = jnp.dot(heads(ona_ref), wna_ref[...], preferred_element_type=F32)
    y_dl = jnp.dot(heads(odl_ref), wdl_ref[...], preferred_element_type=F32)
    merged = jax.nn.sigmoid(heads(gna_ref).astype(F32)) * y_na + jax.nn.sigmoid(heads(gdl_ref).astype(F32)) * y_dl
    mix = jnp.dot(merged.astype(BF16), wo_ref[...], preferred_element_type=F32)
    o_ref[0] = x_ref[0] + gt_ref[0] * mix


def _out_proj(o_na, o_dl, z, x, gt, w_na_out, w_dil_out, w_o):
    b, l, d = x.shape
    tm = min(OUT_TM, l)
    n_gate_heads = d // HEAD_DIM
    assert ZH_G_NA == 0 and ZH_G_DL == n_gate_heads

    def const(shape):
        return pl.BlockSpec(shape, lambda bi, i: (0, 0), pipeline_mode=pl.Buffered(1))

    return pl.pallas_call(
        _out_proj_body,
        grid=(b, l // tm),
        in_specs=[pl.BlockSpec((1, N_HEADS_NA, tm, HEAD_DIM), lambda bi, i: (bi, 0, i, 0)),
                  pl.BlockSpec((1, N_HEADS_PER_DIL, tm, HEAD_DIM), lambda bi, i: (bi, 0, i, 0)),
                  pl.BlockSpec((1, n_gate_heads, tm, HEAD_DIM), lambda bi, i: (bi, 0, i, 0)),
                  pl.BlockSpec((1, n_gate_heads, tm, HEAD_DIM), lambda bi, i: (bi, 1, i, 0)),
                  pl.BlockSpec((1, tm, d), lambda bi, i: (bi, i, 0)),
                  pl.BlockSpec((1, 1, d), lambda bi, i: (bi, 0, 0)),
                  const(w_na_out.shape), const(w_dil_out.shape), const(w_o.shape)],
        out_specs=pl.BlockSpec((1, tm, d), lambda bi, i: (bi, i, 0)),
        out_shape=jax.ShapeDtypeStruct((b, l, d), F32),
        compiler_params=_params("parallel", "parallel"),
        name="out_proj",
    )(o_na, o_dl, z, z, x, gt, w_na_out, w_dil_out, w_o)


def _ffn_body(x_hbm, sh_ref, sc_ref, gt_ref, g_ref, gf_ref, w1_ref, w2_ref, o_ref, u_ref, xbuf, x_sem):
    bi, i, f = pl.program_id(0), pl.program_id(1), pl.program_id(2)
    n_b, n_i = pl.num_programs(0), pl.num_programs(1)
    tm, d = u_ref.shape
    rows = NORM_ROWS

    def x_copy(b_idx, blk):
        return pltpu.make_async_copy(x_hbm.at[pl.ds(b_idx, 1), pl.ds(blk * tm, tm), :], xbuf, x_sem)

    @pl.when(f == 0)
    def _():
        @pl.when((bi == 0) & (i == 0))
        def _():
            x_copy(0, 0).start()
        x_copy(bi, i).wait()

        def init_acc(r0):
            o_ref[0, pl.ds(r0, rows), :] = xbuf[0, pl.ds(r0, rows), :]
        _modulated_norm_rows(xbuf, u_ref, g_ref[...], sc_ref[0], sh_ref[0], on_chunk=init_acc)

    @pl.when(f == 1)
    def _():
        wrap = i == n_i - 1
        nb, ni = jnp.where(wrap, bi + 1, bi), jnp.where(wrap, 0, i + 1)

        @pl.when(nb < n_b)
        def _():
            x_copy(nb, ni).start()

    h = jnp.dot(u_ref[...], w1_ref[...], preferred_element_type=F32)
    h = jnp.square(jnp.maximum(h, 0.0)).astype(BF16)
    for n0 in range(0, d, FFN_TN):
        part = jnp.dot(h, w2_ref[:, n0:n0 + FFN_TN], preferred_element_type=F32)
        o_ref[0, :, n0:n0 + FFN_TN] += gt_ref[0, :, n0:n0 + FFN_TN] * part

    @pl.when(f == pl.num_programs(2) - 1)
    def _():
        def chunk(r, carry):
            r0 = pl.multiple_of(r * rows, rows)
            x2 = o_ref[0, pl.ds(r0, rows), :]
            rinv = lax.rsqrt(jnp.mean(x2 * x2, axis=-1, keepdims=True) + NORM_EPS)
            o_ref[0, pl.ds(r0, rows), :] = o_ref[0, pl.ds(r0, rows), :] * rinv * gf_ref[...]
            return carry
        lax.fori_loop(0, tm // rows, chunk, 0, unroll=2)


def _ffn(x, sh, sc, gt, g_ffn, g_final, w_ff1, w_ff2):
    b, l, d = x.shape
    tm = min(FFN_TM, l)
    tf = FFN_TF
    assert l % tm == 0 and w_ff1.shape[1] // tf >= 2
    return pl.pallas_call(
        _ffn_body,
        grid=(b, l // tm, w_ff1.shape[1] // tf),
        in_specs=[pl.BlockSpec(memory_space=pl.ANY),
                  pl.BlockSpec((1, 1, d), lambda bi, i, f: (bi, 0, 0)),
                  pl.BlockSpec((1, 1, d), lambda bi, i, f: (bi, 0, 0)),
                  pl.BlockSpec((1, 1, d), lambda bi, i, f: (bi, 0, 0)),
                  pl.BlockSpec((1, d), lambda bi, i, f: (0, 0)),
                  pl.BlockSpec((1, d), lambda bi, i, f: (0, 0)),
                  pl.BlockSpec((d, tf), lambda bi, i, f: (0, f)),
                  pl.BlockSpec((tf, d), lambda bi, i, f: (f, 0))],
        out_specs=pl.BlockSpec((1, tm, d), lambda bi, i, f: (bi, i, 0)),
        out_shape=jax.ShapeDtypeStruct((b, l, d), F32),
        scratch_shapes=[pltpu.VMEM((tm, d), BF16), pltpu.VMEM((1, tm, d), F32), pltpu.SemaphoreType.DMA(())],
        compiler_params=_params("arbitrary", "arbitrary", "arbitrary"),
        name="ffn",
    )(x, sh, sc, gt, g_ffn, g_final, w_ff1, w_ff2)


def _layer(x, mod, g_mix, w_in_r, na_tab, dil_tab, w_na_out, w_dil_out, w_o, g_ffn, w_ff1, w_ff2, g_final):
    d = x.shape[-1]
    sh1, sc1, gt1, sh2, sc2, gt2 = [mod[:, None, k * d:(k + 1) * d] for k in range(6)]
    z = _in_proj(x, sh1, sc1, g_mix, w_in_r)
    o_na = _na_attn(z, na_tab)
    o_dl = _dil_attn(z, dil_tab)
    x1 = _out_proj(o_na, o_dl, z, x, gt1, w_na_out, w_dil_out, w_o)
    return _ffn(x1, sh2, sc2, gt2, g_ffn, g_final, w_ff1, w_ff2)


def kernel(x_prompt, x_sample, c_prompt, c_sample, w_mod, b_mod, g_mix, w_in, rpb, w_na_out, w_dil_out, w_o,
           g_ffn, w_ff1, w_ff2, g_final):
    assert w_mod.shape[0] == 1, "single-layer trunk"
    d = x_prompt.shape[-1]
    nb_p, nb_s = c_prompt.shape[0], c_sample.shape[0]
    pad = (-(nb_p + nb_s)) % 8
    c_all = jnp.concatenate([c_prompt, c_sample, jnp.zeros((pad, d), F32)], axis=0)
    mod = _modulation(c_all, w_mod[0], b_mod[0])

    shared = (g_mix[0][None], _w_in_tiles(w_in[0], d), _na_bias_tables(rpb[0]), _dil_bias_tables(),
              w_na_out[0].astype(BF16), w_dil_out[0].astype(BF16), w_o[0].astype(BF16),
              g_ffn[0][None], w_ff1[0].astype(BF16), w_ff2[0].astype(BF16), g_final[None])
    y_prompt = _layer(x_prompt, mod[:nb_p], *shared)
    y_sample = _layer(x_sample, mod[nb_p:nb_p + nb_s], *shared)
    return (y_prompt, y_sample)
```
